```python
import math
import jax, jax.numpy as jnp
from jax import lax
import numpy as np

D_MODEL = 1024
BATCH = 16
SEQ = 2048
DEPTH = 2
DEC_BATCH = 8
DEC_SEQ = 16
PAST_LEN = 2048

CHUNK = 64
D_MIX = D_MODEL
H_A = 8
HEAD_DIM = 64
N_KV = 2
GROUP = H_A // N_KV
H_IDX = 8
D_IDX = 64
TOPK_MAX = 256
Q_BLOCK = 128
NUM_BUCKETS = 32
MAX_DISTANCE = 128
H_B = 4
DK_B = 64
DV_B = 64
C_CONV = 256
CONV_W = 31
D_FF = 2816
EPS = 1e-6

WIDTH_A = H_A * HEAD_DIM
WIDTH_B = H_B * DV_B
WIDTH_C = C_CONV
SPLIT_SIZES = (H_A * HEAD_DIM, N_KV * HEAD_DIM, N_KV * HEAD_DIM, H_IDX * D_IDX, D_IDX, H_IDX,
               H_B * DK_B, H_B * DK_B, H_B * DV_B, H_B * DV_B, 2 * C_CONV)
D_IN = sum(SPLIT_SIZES)
SPLIT_POINTS = tuple(int(v) for v in np.cumsum(SPLIT_SIZES)[:-1])

kernel_name = "hybrid_stream_dsa_hgrn2_conv_step"


def rms_norm(x, g):
    xf = x.astype(jnp.float32)
    y = xf * lax.rsqrt(jnp.mean(xf * xf, axis=-1, keepdims=True) + EPS)
    return (y * g.astype(jnp.float32)).astype(x.dtype)


def layer_norm(x, g, b):
    xf = x.astype(jnp.float32)
    mu = jnp.mean(xf, axis=-1, keepdims=True)
    xc = xf - mu
    y = xc * lax.rsqrt(jnp.mean(xc * xc, axis=-1, keepdims=True) + EPS)
    return (y * g.astype(jnp.float32) + b.astype(jnp.float32)).astype(x.dtype)


def half_ffn(x, norms, w_gu, w_down):
    h = rms_norm(x, norms[0])
    g, u = jnp.split(h @ w_gu, 2, axis=-1)
    return x + 0.5 * rms_norm((jax.nn.silu(g) * u) @ w_down, norms[1])


def t5_bucket(rel):
    half = NUM_BUCKETS // 2
    max_exact = half // 2
    n = jnp.abs(rel)
    big = max_exact + (jnp.log(jnp.maximum(n, 1).astype(jnp.float32) / max_exact)
                       / math.log(MAX_DISTANCE / max_exact) * (half - max_exact)).astype(jnp.int32)
    big = jnp.minimum(big, half - 1)
    return jnp.where(rel > 0, half, 0) + jnp.where(n < max_exact, n, big)


def dsa_block(q, qi, wi, k, v, ki, q_pos, topk, rel_bias):
    B, Q = q.shape[:2]
    L = k.shape[1]
    idx_logits = jnp.einsum('bqhd,bld->bqhl', qi, ki)
    score = jnp.einsum('bqh,bqhl->bql', wi, jax.nn.relu(idx_logits)).astype(jnp.float32)
    k_pos = jnp.arange(L, dtype=jnp.int32)
    admissible = (k_pos[None, :] // CHUNK) <= (q_pos[:, None] // CHUNK)
    score = jnp.where(admissible[None], score, -jnp.inf)
    _, idx = lax.top_k(score, topk)
    gather = jax.vmap(lambda a, i: a[i])
    k_sel = gather(k, idx)
    v_sel = gather(v, idx)
    valid = (idx // CHUNK) <= (q_pos[None, :, None] // CHUNK)
    bias = rel_bias[t5_bucket(idx - q_pos[None, :, None])].astype(jnp.float32)
    bias = jnp.transpose(bias.reshape(B, Q, topk, N_KV, GROUP), (0, 1, 3, 4, 2))
    s = jnp.einsum('bqhgd,bqjhd->bqhgj', q, k_sel).astype(jnp.float32) * (HEAD_DIM ** -0.5) + bias
    s = jnp.where(valid[:, :, None, None, :], s, -jnp.inf)
    p = jax.nn.softmax(s, axis=-1).astype(v.dtype)
    o = jnp.einsum('bqhgj,bqjhd->bqhgd', p, v_sel)
    return o.reshape(B, Q, WIDTH_A)


def dsa_prompt(q, qi, wi, k, v, ki, rel_bias):
    B, L = q.shape[:2]
    nb = L // Q_BLOCK
    topk = min(TOPK_MAX, L // 4)
    blk = lambda a: jnp.moveaxis(a.reshape(B, nb, Q_BLOCK, *a.shape[2:]), 1, 0)
    pos = jnp.arange(L, dtype=jnp.int32).reshape(nb, Q_BLOCK)
    o = lax.map(lambda xs: dsa_block(xs[0], xs[1], xs[2], k, v, ki, xs[3], topk, rel_bias),
                (blk(q), blk(qi), blk(wi), pos))
    return jnp.moveaxis(o, 0, 1).reshape(B, L, WIDTH_A)


def hgrn_inputs(q_b, f_b, i_b, lb):
    B, L = q_b.shape[:2]
    q = jax.nn.silu(q_b.astype(jnp.float32)).reshape(B, L, H_B, DK_B)
    z = f_b.astype(jnp.float32).reshape(B, L, H_B, DK_B)
    lb = lb.reshape(H_B, DK_B)
    log_f = jnp.logaddexp(jnp.log(lb), jnp.log1p(-lb) + jax.nn.log_sigmoid(z))
    k = (1.0 - lb) * jax.nn.sigmoid(-z)
    i = i_b.astype(jnp.float32).reshape(B, L, H_B, DV_B)
    return q, log_f, k, i


def hgrn_chunk(S, q, log_f, k, i):
    L = q.shape[1]
    A = jnp.cumsum(log_f, axis=1)
    causal = jnp.tril(jnp.ones((L, L), dtype=bool))
    diff = A[:, :, None] - A[:, None, :]
    decay = jnp.exp(jnp.where(causal[None, :, :, None, None], diff, -jnp.inf))
    scores = jnp.einsum('bihk,bijhk,bjhk->bhij', q, decay, k)
    o = (jnp.einsum('bhij,bjhv->bihv', scores, i)
         + jnp.einsum('bihk,bhkv->bihv', q * jnp.exp(A), S))
    A_last = A[:, -1]
    S_new = (jnp.exp(A_last)[..., None] * S
             + jnp.einsum('bjhk,bjhv->bhkv', jnp.exp(A_last[:, None] - A) * k, i))
    return S_new, o


def hgrn_prompt(q, log_f, k, i):
    B, L = q.shape[:2]
    nc = L // CHUNK
    to_chunks = lambda a: jnp.moveaxis(a.reshape(B, nc, CHUNK, *a.shape[2:]), 1, 0)
    S0 = jnp.zeros((B, H_B, DK_B, DV_B), jnp.float32)
    S, o = lax.scan(lambda S, xs: hgrn_chunk(S, *xs), S0,
                    (to_chunks(q), to_chunks(log_f), to_chunks(k), to_chunks(i)))
    return S, jnp.moveaxis(o, 0, 1).reshape(B, L, H_B, DV_B)


def hgrn_output(o, g_b, gnorm, dtype):
    B, L = o.shape[:2]
    on = o * lax.rsqrt(jnp.mean(o * o, axis=-1, keepdims=True) + EPS) * gnorm.astype(jnp.float32)
    out = on * jax.nn.silu(g_b.astype(jnp.float32).reshape(B, L, H_B, DV_B))
    return out.reshape(B, L, WIDTH_B).astype(dtype)


def conformer_conv(glu, buf, conv_w, conv_b, ln_g, ln_b):
    a, gate = jnp.split(glu, 2, axis=-1)
    u = a * jax.nn.sigmoid(gate)
    u_pad = jnp.concatenate([buf.astype(u.dtype), u], axis=1)
    y = lax.conv_general_dilated(u_pad, conv_w[:, None, :].astype(u.dtype), window_strides=(1,),
                                 padding='VALID', dimension_numbers=('NWC', 'WIO', 'NWC'),
                                 feature_group_count=C_CONV) + conv_b
    y = jax.nn.silu(layer_norm(y, ln_g, ln_b))
    return y, u_pad[:, -(CONV_W - 1):]


def token_mixer(h, past, w_in, w_out, rel_bias, lb, gnorm, conv_w, conv_b, ln_g, ln_b):
    B, L, _ = h.shape
    (q_a, k_a, v_a, qi, ki, wi, q_b, f_b, i_b, g_b, glu) = jnp.split(h @ w_in, SPLIT_POINTS, axis=-1)
    q_a = q_a.reshape(B, L, N_KV, GROUP, HEAD_DIM)
    k_a = k_a.reshape(B, L, N_KV, HEAD_DIM)
    v_a = v_a.reshape(B, L, N_KV, HEAD_DIM)
    qi = qi.reshape(B, L, H_IDX, D_IDX)
    wi = wi * (H_IDX ** -0.5)
    qb, log_f, kb, ib = hgrn_inputs(q_b, f_b, i_b, lb)
    if past is None:
        o_a = dsa_prompt(q_a, qi, wi, k_a, v_a, ki, rel_bias)
        S_new, o_b = hgrn_prompt(qb, log_f, kb, ib)
        buf = jnp.zeros((B, CONV_W - 1, C_CONV), h.dtype)
    else:
        k_c, v_c, ki_c, S0, buf = past
        k_all = jnp.concatenate([k_c.astype(k_a.dtype), k_a], axis=1)
        v_all = jnp.concatenate([v_c.astype(v_a.dtype), v_a], axis=1)
        ki_all = jnp.concatenate([ki_c.astype(ki.dtype), ki], axis=1)
        q_pos = k_c.shape[1] + jnp.arange(L, dtype=jnp.int32)
        topk = min(TOPK_MAX, k_all.shape[1] // 4)
        o_a = dsa_block(q_a, qi, wi, k_all, v_all, ki_all, q_pos, topk, rel_bias)
        S_new, o_b = hgrn_chunk(S0.astype(jnp.float32), qb, log_f, kb, ib)
    o_b = hgrn_output(o_b, g_b, gnorm, h.dtype)
    o_c, buf_new = conformer_conv(glu, buf, conv_w, conv_b, ln_g, ln_b)
    mix = jnp.concatenate([o_a.astype(h.dtype), o_b, o_c.astype(h.dtype)], axis=-1) @ w_out
    return mix, (k_a, v_a, ki, S_new.astype(h.dtype), buf_new)


def trunk_layer(x, past, norm_ffn1, ffn1_w_gate_up, ffn1_w_down, norm_mix, w_in, w_out, rel_bias, lb, gnorm,
                conv_w, conv_b, conv_ln_g, conv_ln_b, norm_ffn2, ffn2_w_gate_up, ffn2_w_down):
    x = half_ffn(x, norm_ffn1, ffn1_w_gate_up, ffn1_w_down)
    m, st = token_mixer(rms_norm(x, norm_mix[0]), past, w_in, w_out, rel_bias, lb, gnorm,
                        conv_w, conv_b, conv_ln_g, conv_ln_b)
    x = x + rms_norm(m, norm_mix[1])
    x = half_ffn(x, norm_ffn2, ffn2_w_gate_up, ffn2_w_down)
    return x, st


def setup_inputs(seed: int = 0) -> dict:
    key = jax.random.key(seed)
    ks = jax.random.split(key, 32)
    f32 = jnp.float32
    nrm = lambda k, shape, scale: jax.random.normal(k, shape, f32) * scale
    gain = lambda k, shape: 1.0 + 0.05 * jax.random.normal(k, shape, f32)
    return {
        "x_prompt": nrm(ks[0], (BATCH, SEQ, D_MODEL), 1.0),
        "x_sample": nrm(ks[1], (DEC_BATCH, DEC_SEQ, D_MODEL), 1.0),
        "cache_attn_k": nrm(ks[2], (DEPTH, DEC_BATCH, PAST_LEN, N_KV, HEAD_DIM), 1.0),
        "cache_attn_v": nrm(ks[3], (DEPTH, DEC_BATCH, PAST_LEN, N_KV, HEAD_DIM), 1.0),
        "cache_idx_k": nrm(ks[4], (DEPTH, DEC_BATCH, PAST_LEN, D_IDX), 1.0),
        "state_hgrn": nrm(ks[5], (DEPTH, DEC_BATCH, H_B, DK_B, DV_B), 0.5),
        "state_conv": nrm(ks[6], (DEPTH, DEC_BATCH, CONV_W - 1, C_CONV), 0.5),
        "norm_ffn1": gain(ks[7], (DEPTH, 2, D_MODEL)),
        "ffn1_w_gate_up": nrm(ks[8], (DEPTH, D_MODEL, 2 * D_FF), D_MODEL ** -0.5),
        "ffn1_w_down": nrm(ks[9], (DEPTH, D_FF, D_MODEL), D_FF ** -0.5),
        "norm_mix": gain(ks[10], (DEPTH, 2, D_MODEL)),
        "w_in": nrm(ks[11], (DEPTH, D_MODEL, D_IN), D_MODEL ** -0.5),
        "w_out": nrm(ks[12], (DEPTH, D_MIX, D_MODEL), D_MIX ** -0.5),
        "rel_bias": nrm(ks[13], (NUM_BUCKETS, H_A), 0.5),
        "hgrn_lb_logits": nrm(ks[14], (DEPTH, H_B * DK_B), 1.0),
        "hgrn_gnorm": gain(ks[15], (DEPTH, DV_B)),
        "conv_w": nrm(ks[16], (DEPTH, CONV_W, C_CONV), CONV_W ** -0.5),
        "conv_b": nrm(ks[17], (DEPTH, C_CONV), 0.02),
        "conv_ln_g": gain(ks[18], (DEPTH, C_CONV)),
        "conv_ln_b": nrm(ks[19], (DEPTH, C_CONV), 0.02),
        "norm_ffn2": gain(ks[20], (DEPTH, 2, D_MODEL)),
        "ffn2_w_gate_up": nrm(ks[21], (DEPTH, D_MODEL, 2 * D_FF), D_MODEL ** -0.5),
        "ffn2_w_down": nrm(ks[22], (DEPTH, D_FF, D_MODEL), D_FF ** -0.5),
    }


def reference(x_prompt, x_sample, cache_attn_k, cache_attn_v, cache_idx_k, state_hgrn, state_conv,
              norm_ffn1, ffn1_w_gate_up, ffn1_w_down, norm_mix, w_in, w_out, rel_bias,
              hgrn_lb_logits, hgrn_gnorm, conv_w, conv_b, conv_ln_g, conv_ln_b,
              norm_ffn2, ffn2_w_gate_up, ffn2_w_down):
    p_lb = jax.nn.softmax(hgrn_lb_logits.astype(jnp.float32), axis=0)
    lb_all = jnp.cumsum(p_lb, axis=0)
    lb_all = lb_all - lb_all[0]
    xp, xs = x_prompt, x_sample
    st_prompt = ([], [], [], [], [])
    st_sample = ([], [], [], [], [])
    for l in range(DEPTH):
        params = (norm_ffn1[l], ffn1_w_gate_up[l], ffn1_w_down[l], norm_mix[l], w_in[l], w_out[l], rel_bias,
                  lb_all[l], hgrn_gnorm[l], conv_w[l], conv_b[l], conv_ln_g[l], conv_ln_b[l],
                  norm_ffn2[l], ffn2_w_gate_up[l], ffn2_w_down[l])
        xp, sp = trunk_layer(xp, None, *params)
        past = (cache_attn_k[l], cache_attn_v[l], cache_idx_k[l], state_hgrn[l], state_conv[l])
        xs, ss = trunk_layer(xs, past, *params)
        for j in range(5):
            st_prompt[j].append(sp[j])
            st_sample[j].append(ss[j])
    k_p, v_p, ki_p, s_p, c_p = [jnp.stack(a, axis=0) for a in st_prompt]
    k_s, v_s, ki_s, s_s, c_s = [jnp.stack(a, axis=0) for a in st_sample]
    return (xp, xs, k_p, v_p, ki_p, s_p, c_p, k_s, v_s, ki_s, s_s, c_s)
```

```python
import functools
import math

import jax
import jax.numpy as jnp
from jax import lax
from jax.experimental import pallas as pl
from jax.experimental.pallas import tpu as pltpu

D_MODEL = 1024
DEPTH = 2
CHUNK = 64
CHUNK_SHIFT = 6
H_A = 8
HEAD_DIM = 64
N_KV = 2
GROUP = H_A // N_KV
H_IDX = 8
D_IDX = 64
TOPK_MAX = 256
NUM_BUCKETS = 32
MAX_DISTANCE = 128
H_B = 4
DK_B = 64
DV_B = 64
C_CONV = 256
CONV_W = 31
D_FF = 2816
EPS = 1e-6

WIDTH_A = H_A * HEAD_DIM
WIDTH_B = H_B * DV_B
WIDTH_BC = WIDTH_B + C_CONV
KV_W = N_KV * HEAD_DIM
IQ_W = H_IDX * D_IDX
BC_W = 4 * WIDTH_B + 2 * C_CONV

F32 = jnp.float32
BF16 = jnp.bfloat16
I32 = jnp.int32

QB = 128
SUB = 16
HALO = 32
FFN_CHUNK = 256
VMEM_LIMIT = 56 * 1024 * 1024
INT_MIN = -2 ** 31
NEG_BIG = -1e30

NT_DIMS = (((1,), (1,)), ((), ()))
TN_DIMS = (((0,), (0,)), ((), ()))


def _bucket_thresholds():
    half = NUM_BUCKETS // 2
    max_exact = half // 2

    def big(n):
        return min(max_exact + int(math.log(n / max_exact) / math.log(MAX_DISTANCE / max_exact) * (half - max_exact)),
                   half - 1)

    return tuple(min(n for n in range(max_exact, 4 * MAX_DISTANCE) if big(n) >= b) for b in range(max_exact + 1, half))


BUCKET_THRESHOLDS = _bucket_thresholds()


def _params(*sem):
    return pltpu.CompilerParams(dimension_semantics=sem, vmem_limit_bytes=VMEM_LIMIT)


def _resident(shape):
    return pl.BlockSpec(shape, lambda *_: (0,) * len(shape), pipeline_mode=pl.Buffered(1))


def _rms(x, g):
    return x * lax.rsqrt(jnp.mean(x * x, axis=-1, keepdims=True) + EPS) * g


def _sigmoid(x):
    return 1.0 / (1.0 + jnp.exp(-x))


def _ffn_body(x_ref, n_ref, wgu_ref, wd_ref, o_ref, acc_ref):
    x = x_ref[...]
    h = _rms(x, n_ref[0:1, :]).astype(BF16)
    for c in range(D_FF // FFN_CHUNK):
        lo, hi = c * FFN_CHUNK, (c + 1) * FFN_CHUNK
        g = jnp.dot(h, wgu_ref[:, lo:hi], preferred_element_type=F32)
        u = jnp.dot(h, wgu_ref[:, D_FF + lo:D_FF + hi], preferred_element_type=F32)
        a = (g * _sigmoid(g) * u).astype(BF16)
        part = jnp.dot(a, wd_ref[lo:hi, :], preferred_element_type=F32)
        if c == 0:
            acc_ref[...] = part
        else:
            acc_ref[...] += part
    o_ref[...] = x + 0.5 * _rms(acc_ref[...], n_ref[1:2, :])


def _ffn_half(x, norms, w_gu, w_down, tm):
    t = x.shape[0]
    return pl.pallas_call(
        _ffn_body,
        grid=(t // tm,),
        in_specs=[pl.BlockSpec((tm, D_MODEL), lambda i: (i, 0)),
                  _resident((2, D_MODEL)),
                  _resident((D_MODEL, 2 * D_FF)),
                  _resident((D_FF, D_MODEL))],
        out_specs=pl.BlockSpec((tm, D_MODEL), lambda i: (i, 0)),
        out_shape=jax.ShapeDtypeStruct((t, D_MODEL), F32),
        scratch_shapes=[pltpu.VMEM((tm, D_MODEL), F32)],
        compiler_params=_params("arbitrary"),
        name="ffn_half",
    )(x, norms, w_gu, w_down)


def _inproj_body(x_ref, g_ref, wq_ref, wqi_ref, wwi_ref, wk_ref, wv_ref, wki_ref, wbc_ref,
                 qt_ref, qit_ref, wit_ref, k_ref, v_ref, ki_ref, pbc_ref):
    h = _rms(x_ref[...], g_ref[...]).astype(BF16)
    qt = lax.dot_general(wq_ref[...], h, NT_DIMS, preferred_element_type=F32)
    qt_ref[...] = (qt * (HEAD_DIM ** -0.5)).astype(BF16)
    qit_ref[...] = lax.dot_general(wqi_ref[...], h, NT_DIMS, preferred_element_type=F32).astype(BF16)
    wit_ref[...] = lax.dot_general(wwi_ref[...], h, NT_DIMS, preferred_element_type=F32) * (H_IDX ** -0.5)
    k_ref[...] = jnp.dot(h, wk_ref[...], preferred_element_type=F32)
    v_ref[...] = jnp.dot(h, wv_ref[...], preferred_element_type=F32)
    ki_ref[...] = jnp.dot(h, wki_ref[...], preferred_element_type=F32)
    pbc_ref[...] = jnp.dot(h, wbc_ref[...], preferred_element_type=F32)


def _in_proj(x, g, w, tm):
    t = x.shape[0]
    row = lambda i: (i, 0)
    col = lambda i: (0, i)
    return pl.pallas_call(
        _inproj_body,
        grid=(t // tm,),
        in_specs=[pl.BlockSpec((tm, D_MODEL), row),
                  _resident((1, D_MODEL)),
                  _resident((WIDTH_A, D_MODEL)),
                  _resident((IQ_W, D_MODEL)),
                  _resident((H_IDX, D_MODEL)),
                  _resident((D_MODEL, KV_W)),
                  _resident((D_MODEL, KV_W)),
                  _resident((D_MODEL, D_IDX)),
                  _resident((D_MODEL, BC_W))],
        out_specs=[pl.BlockSpec((WIDTH_A, tm), col),
                   pl.BlockSpec((IQ_W, tm), col),
                   pl.BlockSpec((H_IDX, tm), col),
                   pl.BlockSpec((tm, KV_W), row),
                   pl.BlockSpec((tm, KV_W), row),
                   pl.BlockSpec((tm, D_IDX), row),
                   pl.BlockSpec((tm, BC_W), row)],
        out_shape=[jax.ShapeDtypeStruct((WIDTH_A, t), BF16),
                   jax.ShapeDtypeStruct((IQ_W, t), BF16),
                   jax.ShapeDtypeStruct((H_IDX, t), F32),
                   jax.ShapeDtypeStruct((t, KV_W), F32),
                   jax.ShapeDtypeStruct((t, KV_W), F32),
                   jax.ShapeDtypeStruct((t, D_IDX), F32),
                   jax.ShapeDtypeStruct((t, BC_W), F32)],
        compiler_params=_params("arbitrary"),
        name="in_proj",
    )(x, g, w["q_t"], w["qi_t"], w["wi_t"], w["k"], w["v"], w["ki"], w["bc"])


def _bias_body(rb_ref, o_ref):
    j = lax.broadcasted_iota(I32, (QB, QB), 0)
    i = lax.broadcasted_iota(I32, (QB, QB), 1)
    half = NUM_BUCKETS // 2
    max_exact = half // 2
    for var, off in ((1, -QB), (2, 0)):
        rel = j - i + off
        n = jnp.abs(rel)
        big = jnp.full((QB, QB), max_exact, I32)
        for thr in BUCKET_THRESHOLDS:
            big = big + jnp.where(n >= thr, 1, 0)
        bucket = jnp.where(n < max_exact, n, big) + jnp.where(rel > 0, half, 0)
        for head in range(H_A):
            g, hh = divmod(head, GROUP)
            tile = jnp.zeros((QB, QB), F32)
            for b in range(NUM_BUCKETS):
                tile = jnp.where(bucket == b, rb_ref[b, head], tile)
            o_ref[g, var, :, hh * QB:(hh + 1) * QB] = tile
    for head in range(H_A):
        g, hh = divmod(head, GROUP)
        o_ref[g, 0, :, hh * QB:(hh + 1) * QB] = jnp.full((QB, QB), rb_ref[half - 1, head], F32)


def _bias_tiles(rel_bias):
    return pl.pallas_call(
        _bias_body,
        in_specs=[pl.BlockSpec(memory_space=pltpu.SMEM)],
        out_shape=jax.ShapeDtypeStruct((N_KV, 3, QB, GROUP * QB), F32),
        name="rel_bias_tiles",
    )(rel_bias)


def _dsa_body(qt_ref, qit_ref, wit_ref, k_ref, v_ref, ki_ref, bias_ref, o_ref, keys_ref, acc_ref, *,
              q_base, tile_base, l_valid, topk, idx_bits):
    qb = pl.program_id(1)
    nt = tile_base + qb + 1
    q_chunk = (q_base + qb * QB + lax.broadcasted_iota(I32, (QB, QB), 1)) >> CHUNK_SHIFT
    row = lax.broadcasted_iota(I32, (QB, QB), 0)

    def admissible(r0):
        k_pos = r0 + row
        return jnp.where(k_pos < l_valid, k_pos >> CHUNK_SHIFT, 2 ** 30) <= q_chunk

    qi_cat = jnp.concatenate([qit_ref[h * D_IDX:(h + 1) * D_IDX, :] for h in range(H_IDX)], axis=1)
    wi_cat = jnp.concatenate([wit_ref[h:h + 1, :] for h in range(H_IDX)], axis=1)

    def score_tile(kt, carry):
        r0 = pl.multiple_of(kt * QB, QB)
        ki = ki_ref[pl.ds(r0, QB), :].astype(BF16)
        lg = jnp.dot(ki, qi_cat, preferred_element_type=F32)
        w = jnp.maximum(lg, 0.0) * wi_cat
        sc = w[:, 0:QB]
        for h in range(1, H_IDX):
            sc = sc + w[:, h * QB:(h + 1) * QB]
        sc = jnp.where(admissible(r0), sc + 0.0, -jnp.inf)
        bits = pltpu.bitcast(sc, I32)
        keys_ref[pl.ds(r0, QB), :] = bits ^ ((bits >> 31) & 0x7FFFFFFF)
        return carry

    lax.fori_loop(0, nt, score_tile, 0)

    @pl.when(nt % 2 == 1)
    def _():
        keys_ref[pl.ds(pl.multiple_of(nt * QB, QB), QB), :] = jnp.full((QB, QB), INT_MIN, I32)

    n2 = (nt + 1) // 2

    def count(hit_fn):
        def body(t, acc):
            r0 = pl.multiple_of(t * 2 * QB, 2 * QB)
            hit = hit_fn(keys_ref[pl.ds(r0, 2 * QB), :], r0)
            return acc + hit.reshape(2 * QB // 8, 8, QB).sum(axis=0)

        acc = lax.fori_loop(0, n2, body, jnp.zeros((8, QB), I32))
        return acc.sum(axis=0, keepdims=True)

    def count_ge(cand):
        return count(lambda blk, r0: jnp.where(blk >= cand, 1, 0))

    thr = jnp.where(count_ge(jnp.zeros((1, QB), I32)) >= topk, 0, INT_MIN)

    def bit_step(bi, thr):
        cand = thr + lax.shift_left(jnp.int32(1), 30 - bi)
        return jnp.where(count_ge(cand) >= topk, cand, thr)

    thr = lax.fori_loop(0, 31, bit_step, thr)

    need = topk - count_ge(thr + 1)
    row2 = lax.broadcasted_iota(I32, (2 * QB, QB), 0)

    def count_tie_below(cand):
        return count(lambda blk, r0: jnp.where(blk == thr, jnp.where(r0 + row2 < cand, 1, 0), 0))

    def idx_step(bi, last):
        cand = last + lax.shift_left(jnp.int32(1), idx_bits - 1 - bi)
        return jnp.where(count_tie_below(cand) < need, cand, last)

    last = lax.fori_loop(0, idx_bits, idx_step, jnp.zeros((1, QB), I32))

    q_cat = [jnp.concatenate([qt_ref[(g * GROUP + hh) * HEAD_DIM:(g * GROUP + hh + 1) * HEAD_DIM, :]
                              for hh in range(GROUP)], axis=1) for g in range(N_KV)]
    acc_ref[...] = jnp.zeros(acc_ref.shape, F32)

    def att_tile(kt, carry):
        r0 = pl.multiple_of(kt * QB, QB)
        key = keys_ref[pl.ds(r0, QB), :]
        picked = jnp.where(key > thr, 0.0, jnp.where(key == thr, jnp.where(r0 + row <= last, 0.0, -jnp.inf), -jnp.inf))
        madd = jnp.where(admissible(r0), picked, -jnp.inf)
        madd = jnp.concatenate([madd] * GROUP, axis=1)
        bias_idx = jnp.clip(kt - nt + 3, 0, 2)
        kk = k_ref[pl.ds(r0, QB), :].astype(BF16)
        vt = v_ref[pl.ds(r0, QB), :].T.astype(BF16)
        out = []
        for g in range(N_KV):
            m_old, l_old = carry[2 * g], carry[2 * g + 1]
            s = jnp.dot(kk[:, g * HEAD_DIM:(g + 1) * HEAD_DIM], q_cat[g], preferred_element_type=F32)
            s = s + (bias_ref[g, bias_idx] + madd)
            m_new = jnp.maximum(m_old, s.max(axis=0, keepdims=True))
            alpha = jnp.exp(m_old - m_new)
            p = jnp.exp(s - m_new)
            l_new = alpha * l_old + p.sum(axis=0, keepdims=True)
            pv = jnp.dot(vt[g * HEAD_DIM:(g + 1) * HEAD_DIM, :], p.astype(BF16), preferred_element_type=F32)
            acc_ref[g] = acc_ref[g] * alpha + pv
            out += [m_new, l_new]
        return tuple(out)

    init = (jnp.full((1, GROUP * QB), NEG_BIG, F32), jnp.zeros((1, GROUP * QB), F32)) * N_KV
    fin = lax.fori_loop(0, nt, att_tile, init)
    heads = []
    for g in range(N_KV):
        og = acc_ref[g] / fin[2 * g + 1]
        heads += [og[:, hh * QB:(hh + 1) * QB] for hh in range(GROUP)]
    o_ref[...] = jnp.concatenate(heads, axis=0).T


def _dsa(qt, qit, wit, k, v, ki, bias, *, n_qb, q_base, tile_base, l_valid):
    b, lk, _ = k.shape
    n_tiles_max = tile_base + n_qb
    assert lk >= n_tiles_max * QB
    keys_rows = (n_tiles_max + 1) // 2 * 2 * QB
    topk = min(TOPK_MAX, l_valid // 4)
    body = functools.partial(_dsa_body, q_base=q_base, tile_base=tile_base, l_valid=l_valid, topk=topk,
                             idx_bits=max(1, (keys_rows - 1).bit_length()))
    tok = lambda bi, qi: (0, bi * n_qb + qi)
    per_batch = lambda bi, qi: (bi, 0, 0)
    return pl.pallas_call(
        body,
        grid=(b, n_qb),
        in_specs=[pl.BlockSpec((WIDTH_A, QB), tok),
                  pl.BlockSpec((IQ_W, QB), tok),
                  pl.BlockSpec((H_IDX, QB), tok),
                  pl.BlockSpec((None, lk, KV_W), per_batch),
                  pl.BlockSpec((None, lk, KV_W), per_batch),
                  pl.BlockSpec((None, lk, D_IDX), per_batch),
                  _resident((N_KV, 3, QB, GROUP * QB))],
        out_specs=pl.BlockSpec((QB, WIDTH_A), lambda bi, qi: (bi * n_qb + qi, 0)),
        out_shape=jax.ShapeDtypeStruct((b * n_qb * QB, WIDTH_A), F32),
        scratch_shapes=[pltpu.VMEM((keys_rows, QB), I32),
                        pltpu.VMEM((N_KV, HEAD_DIM, GROUP * QB), F32)],
        compiler_params=_params("arbitrary", "arbitrary"),
        name="dsa_attention",
    )(qt, qit, wit, k, v, ki, bias)


def _hgrn_conv_body(pbc_ref, lbl_ref, gn_ref, cw_ref, cb_ref, lng_ref, lnb_ref, s0_ref, cbuf_ref,
                    obc_ref, sout_ref, cout_ref,
                    st_ref, ubuf_ref, xbuf_ref, r_ref, a_ref, qs_ref, kk_ref, qd_ref, kd_ref, dd_ref, oh_ref, *,
                    layer, tm):
    t = pl.program_id(1)
    n_sub = tm // SUB
    hw = H_B * DK_B
    ri = lax.broadcasted_iota(I32, (hw, hw), 0)
    ci = lax.broadcasted_iota(I32, (hw, hw), 1)
    same_head = (ri // DK_B) == (ci // DK_B)
    bd = jnp.where(same_head, 1.0, 0.0)

    @pl.when(t == 0)
    def _():
        s_bd = jnp.concatenate([s0_ref[...]] * H_B, axis=1) * bd
        st_ref[...] = s_bd.T
        ubuf_ref[0:HALO - (CONV_W - 1), :] = jnp.zeros((HALO - (CONV_W - 1), C_CONV), F32)
        ubuf_ref[HALO - (CONV_W - 1):HALO, :] = cbuf_ref[...]

    @pl.when(t > 0)
    def _():
        ubuf_ref[0:HALO, :] = ubuf_ref[tm:tm + HALO, :]

    lg = lbl_ref[...]
    e = jnp.exp(lg - lg.max(axis=0, keepdims=True))
    p = e / e.sum(axis=0, keepdims=True)
    lb = jnp.zeros((1, hw), F32)
    for m in range(1, layer + 1):
        lb = lb + p[m:m + 1, :]

    qraw = pbc_ref[:, 0:hw]
    z = pbc_ref[:, hw:2 * hw]
    qs_ref[...] = qraw * _sigmoid(qraw)
    log_sig = -(jnp.maximum(-z, 0.0) + jnp.log1p(jnp.exp(-jnp.abs(z))))
    a1 = jnp.log(lb)
    b1 = jnp.log1p(-lb) + log_sig
    log_f = jnp.maximum(a1, b1) + jnp.log1p(jnp.exp(-jnp.abs(a1 - b1)))
    kk_ref[...] = (1.0 - lb) * _sigmoid(-z)

    tr = lax.broadcasted_iota(I32, (tm, tm), 0)
    tc = lax.broadcasted_iota(I32, (tm, tm), 1)
    same_sub = (tr // SUB) == (tc // SUB)
    tri = jnp.where(same_sub, jnp.where(tc <= tr, 1.0, 0.0), 0.0)
    blk = jnp.where(same_sub, 1.0, 0.0)
    a_loc = jnp.dot(tri, log_f, preferred_element_type=F32, precision=lax.Precision.HIGHEST)
    a_end = jnp.dot(blk, log_f, preferred_element_type=F32, precision=lax.Precision.HIGHEST)
    a_ref[...] = a_loc
    qd_ref[...] = (qs_ref[...] * jnp.exp(a_loc)).astype(BF16)
    kd_ref[...] = (kk_ref[...] * jnp.exp(a_end - a_loc)).astype(BF16)
    dd_ref[...] = jnp.exp(a_end)

    ii = lax.broadcasted_iota(I32, (SUB, hw), 0)

    def build_x(s, carry):
        r0 = pl.multiple_of(s * SUB, SUB)
        a_s = a_ref[pl.ds(r0, SUB), :]
        q_s = qs_ref[pl.ds(r0, SUB), :]
        for j in range(SUB):
            a_j = a_ref[pl.ds(r0 + j, 1), :]
            k_j = kk_ref[pl.ds(r0 + j, 1), :]
            dec = jnp.exp(jnp.where(ii >= j, a_s - a_j, -jnp.inf))
            x0 = pl.multiple_of(s * SUB * SUB + j * SUB, SUB)
            xbuf_ref[pl.ds(x0, SUB), :] = (q_s * k_j * dec).astype(BF16)
        return carry

    lax.fori_loop(0, n_sub, build_x, 0)
    r_ref[...] = jnp.dot(xbuf_ref[...], bd.astype(BF16), preferred_element_type=F32)

    def sub_step(s, carry):
        r0 = pl.multiple_of(s * SUB, SUB)
        o = jnp.zeros((SUB, hw), F32)
        for j in range(SUB):
            x0 = pl.multiple_of(s * SUB * SUB + j * SUB, SUB)
            o = o + r_ref[pl.ds(x0, SUB), :] * pbc_ref[pl.ds(r0 + j, 1), 2 * hw:3 * hw]
        st = st_ref[...]
        o = o + lax.dot_general(qd_ref[pl.ds(r0, SUB), :], st.astype(BF16), NT_DIMS, preferred_element_type=F32)
        oh_ref[pl.ds(r0, SUB), :] = o
        val = pbc_ref[pl.ds(r0, SUB), 2 * hw:3 * hw].astype(BF16)
        ut = lax.dot_general(val, kd_ref[pl.ds(r0, SUB), :], TN_DIMS, preferred_element_type=F32)
        st_ref[...] = st * dd_ref[pl.ds(r0, 1), :] + ut * bd
        return carry

    lax.fori_loop(0, n_sub, sub_step, 0)

    o = oh_ref[...]
    ms = jnp.dot(o * o, bd, preferred_element_type=F32, precision=lax.Precision.HIGHEST) * (1.0 / DV_B)
    gate = pbc_ref[:, 3 * hw:4 * hw]
    obc_ref[:, 0:WIDTH_B] = o * lax.rsqrt(ms + EPS) * gn_ref[...] * (gate * _sigmoid(gate))

    c0 = 4 * hw
    u = pbc_ref[:, c0:c0 + C_CONV] * _sigmoid(pbc_ref[:, c0 + C_CONV:c0 + 2 * C_CONV])
    ubuf_ref[HALO:HALO + tm, :] = u
    y = jnp.zeros((tm, C_CONV), F32)
    for w in range(CONV_W):
        y = y + ubuf_ref[pl.ds(HALO - (CONV_W - 1) + w, tm), :] * cw_ref[w:w + 1, :]
    y = y + cb_ref[...]
    yc = y - jnp.mean(y, axis=-1, keepdims=True)
    ln = yc * lax.rsqrt(jnp.mean(yc * yc, axis=-1, keepdims=True) + EPS) * lng_ref[...] + lnb_ref[...]
    obc_ref[:, WIDTH_B:WIDTH_BC] = ln * _sigmoid(ln)

    @pl.when(t == pl.num_programs(1) - 1)
    def _():
        s_bd = st_ref[...].T
        s_new = s_bd[:, 0:DV_B]
        for h in range(1, H_B):
            s_new = s_new + s_bd[:, h * DV_B:(h + 1) * DV_B]
        sout_ref[...] = s_new
        cout_ref[...] = ubuf_ref[tm + HALO - (CONV_W - 1):tm + HALO, :]


def _hgrn_conv(pbc, lb_logits, gn, cw, cb, lng, lnb, s0, cbuf, *, layer, seq, tm):
    b = s0.shape[0]
    n_t = seq // tm
    n_sub = tm // SUB
    hw = H_B * DK_B
    body = functools.partial(_hgrn_conv_body, layer=layer, tm=tm)
    per_batch = lambda bi, ti: (bi, 0, 0)
    tok = lambda bi, ti: (bi * n_t + ti, 0)
    return pl.pallas_call(
        body,
        grid=(b, n_t),
        in_specs=[pl.BlockSpec((tm, BC_W), tok),
                  _resident((DEPTH, hw)),
                  _resident((1, hw)),
                  _resident((CONV_W, C_CONV)),
                  _resident((1, C_CONV)),
                  _resident((1, C_CONV)),
                  _resident((1, C_CONV)),
                  pl.BlockSpec((None, hw, DV_B), per_batch),
                  pl.BlockSpec((None, CONV_W - 1, C_CONV), per_batch)],
        out_specs=[pl.BlockSpec((tm, WIDTH_BC), tok),
                   pl.BlockSpec((None, hw, DV_B), per_batch),
                   pl.BlockSpec((None, CONV_W - 1, C_CONV), per_batch)],
        out_shape=[jax.ShapeDtypeStruct((b * seq, WIDTH_BC), F32),
                   jax.ShapeDtypeStruct((b, hw, DV_B), F32),
                   jax.ShapeDtypeStruct((b, CONV_W - 1, C_CONV), F32)],
        scratch_shapes=[pltpu.VMEM((hw, hw), F32),
                        pltpu.VMEM((tm + HALO, C_CONV), F32),
                        pltpu.VMEM((n_sub * SUB * SUB, hw), BF16),
                        pltpu.VMEM((n_sub * SUB * SUB, hw), F32),
                        pltpu.VMEM((tm, hw), F32),
                        pltpu.VMEM((tm, hw), F32),
                        pltpu.VMEM((tm, hw), F32),
                        pltpu.VMEM((tm, hw), BF16),
                        pltpu.VMEM((tm, hw), BF16),
                        pltpu.VMEM((tm, hw), F32),
                        pltpu.VMEM((tm, hw), F32)],
        compiler_params=_params("arbitrary", "arbitrary"),
        name="hgrn_conv",
    )(pbc, lb_logits, gn, cw, cb, lng, lnb, s0, cbuf)


def _outproj_body(x_ref, oa_ref, obc_ref, wa_ref, wbc_ref, g_ref, o_ref):
    m = jnp.dot(oa_ref[...].astype(BF16), wa_ref[...], preferred_element_type=F32)
    m = m + jnp.dot(obc_ref[...].astype(BF16), wbc_ref[...], preferred_element_type=F32)
    o_ref[...] = x_ref[...] + _rms(m, g_ref[...])


def _out_proj(x, oa, obc, wa, wbc, g, tm):
    t = x.shape[0]
    row = lambda i: (i, 0)
    return pl.pallas_call(
        _outproj_body,
        grid=(t // tm,),
        in_specs=[pl.BlockSpec((tm, D_MODEL), row),
                  pl.BlockSpec((tm, WIDTH_A), row),
                  pl.BlockSpec((tm, WIDTH_BC), row),
                  _resident((WIDTH_A, D_MODEL)),
                  _resident((WIDTH_BC, D_MODEL)),
                  _resident((1, D_MODEL))],
        out_specs=pl.BlockSpec((tm, D_MODEL), row),
        out_shape=jax.ShapeDtypeStruct((t, D_MODEL), F32),
        compiler_params=_params("arbitrary"),
        name="out_proj",
    )(x, oa, obc, wa, wbc, g)


def _split_w_in(w_in):
    sizes = (WIDTH_A, KV_W, KV_W, IQ_W, D_IDX, H_IDX)
    offs = [0]
    for s in sizes:
        offs.append(offs[-1] + s)
    wb = w_in.astype(BF16)
    q, k, v, qi, ki, wi = (wb[:, offs[n]:offs[n + 1]] for n in range(len(sizes)))
    return {"q_t": q.T, "qi_t": qi.T, "wi_t": wi.T, "k": k, "v": v, "ki": ki, "bc": wb[:, offs[-1]:]}


def _layer(x, past, lw, bias, *, layer, batch, seq, tm_tok, tm_seq):
    x = _ffn_half(x, lw["norm_ffn1"], lw["ffn1_gu"], lw["ffn1_down"], tm_tok)
    qt, qit, wit, k_new, v_new, ki_new, pbc = _in_proj(x, lw["norm_mix"][0:1], lw["w_in"], tm_tok)
    hw = H_B * DK_B
    if past is None:
        o_a = _dsa(qt, qit, wit, k_new.reshape(batch, seq, KV_W), v_new.reshape(batch, seq, KV_W),
                   ki_new.reshape(batch, seq, D_IDX), bias, n_qb=seq // QB, q_base=0, tile_base=0, l_valid=seq)
        s0 = jnp.zeros((batch, hw, DV_B), F32)
        cbuf = jnp.zeros((batch, CONV_W - 1, C_CONV), F32)
    else:
        k_c, v_c, ki_c, s0, cbuf = past
        past_len = k_c.shape[1]
        assert past_len % QB == 0 and seq <= QB
        l_valid = past_len + seq
        pad_k = lambda c, n, w: jnp.pad(jnp.concatenate([c.reshape(batch, past_len, w), n.reshape(batch, seq, w)], axis=1),
                                        ((0, 0), (0, QB - seq), (0, 0)))
        pad_q = lambda a: jnp.pad(a.reshape(a.shape[0], batch, seq), ((0, 0), (0, 0), (0, QB - seq))).reshape(a.shape[0], batch * QB)
        o_a = _dsa(pad_q(qt), pad_q(qit), pad_q(wit), pad_k(k_c, k_new, KV_W), pad_k(v_c, v_new, KV_W),
                   pad_k(ki_c, ki_new, D_IDX), bias, n_qb=1, q_base=past_len, tile_base=past_len // QB, l_valid=l_valid)
        o_a = o_a.reshape(batch, QB, WIDTH_A)[:, :seq].reshape(batch * seq, WIDTH_A)
        s0 = s0.reshape(batch, hw, DV_B)
    obc, s_new, c_new = _hgrn_conv(pbc, lw["lb_logits"], lw["gnorm"], lw["conv_w"], lw["conv_b"], lw["ln_g"], lw["ln_b"],
                                   s0, cbuf, layer=layer, seq=seq, tm=tm_seq)
    x = _out_proj(x, o_a, obc, lw["w_out_a"], lw["w_out_bc"], lw["norm_mix"][1:2], tm_tok)
    x = _ffn_half(x, lw["norm_ffn2"], lw["ffn2_gu"], lw["ffn2_down"], tm_tok)
    state = (k_new.reshape(batch, seq, N_KV, HEAD_DIM), v_new.reshape(batch, seq, N_KV, HEAD_DIM),
             ki_new.reshape(batch, seq, D_IDX), s_new.reshape(batch, H_B, DK_B, DV_B), c_new)
    return x, state


def kernel(x_prompt, x_sample, cache_attn_k, cache_attn_v, cache_idx_k, state_hgrn, state_conv, norm_ffn1, ffn1_w_gate_up, ffn1_w_down, norm_mix, w_in, w_out, rel_bias, hgrn_lb_logits, hgrn_gnorm, conv_w, conv_b, conv_ln_g, conv_ln_b, norm_ffn2, ffn2_w_gate_up, ffn2_w_down):
    bp, sp, _ = x_prompt.shape
    bs, ss, _ = x_sample.shape
    bias = _bias_tiles(rel_bias)
    xp = x_prompt.reshape(bp * sp, D_MODEL)
    xs = x_sample.reshape(bs * ss, D_MODEL)
    st_p, st_s = [], []
    for l in range(DEPTH):
        lw = {
            "norm_ffn1": norm_ffn1[l], "ffn1_gu": ffn1_w_gate_up[l].astype(BF16), "ffn1_down": ffn1_w_down[l].astype(BF16),
            "norm_mix": norm_mix[l], "w_in": _split_w_in(w_in[l]),
            "w_out_a": w_out[l, :WIDTH_A].astype(BF16), "w_out_bc": w_out[l, WIDTH_A:].astype(BF16),
            "lb_logits": hgrn_lb_logits, "gnorm": jnp.tile(hgrn_gnorm[l], H_B)[None, :],
            "conv_w": conv_w[l], "conv_b": conv_b[l][None, :], "ln_g": conv_ln_g[l][None, :], "ln_b": conv_ln_b[l][None, :],
            "norm_ffn2": norm_ffn2[l], "ffn2_gu": ffn2_w_gate_up[l].astype(BF16), "ffn2_down": ffn2_w_down[l].astype(BF16),
        }
        xp, sp_state = _layer(xp, None, lw, bias, layer=l, batch=bp, seq=sp, tm_tok=min(512, bp * sp), tm_seq=min(256, sp))
        past = (cache_attn_k[l], cache_attn_v[l], cache_idx_k[l], state_hgrn[l], state_conv[l])
        xs, ss_state = _layer(xs, past, lw, bias, layer=l, batch=bs, seq=ss, tm_tok=min(512, bs * ss), tm_seq=min(256, ss))
        st_p.append(sp_state)
        st_s.append(ss_state)
    stack = lambda sts, j: jnp.stack([s[j] for s in sts], axis=0)
    return (xp.reshape(bp, sp, D_MODEL), xs.reshape(bs, ss, D_MODEL),
            *(stack(st_p, j) for j in range(5)), *(stack(st_s, j) for j in range(5)))
```

```python
import functools
import math

import jax
import jax.numpy as jnp
from jax import lax
from jax.experimental import pallas as pl
from jax.experimental.pallas import tpu as pltpu

D_MODEL = 1024
DEPTH = 2
CHUNK = 64
CHUNK_SHIFT = 6
H_A = 8
HEAD_DIM = 64
N_KV = 2
GROUP = H_A // N_KV
H_IDX = 8
D_IDX = 64
TOPK_MAX = 256
NUM_BUCKETS = 32
MAX_DISTANCE = 128
H_B = 4
DK_B = 64
DV_B = 64
C_CONV = 256
CONV_W = 31
D_FF = 2816
EPS = 1e-6

WIDTH_A = H_A * HEAD_DIM
WIDTH_B = H_B * DV_B
WIDTH_BC = WIDTH_B + C_CONV
KV_W = N_KV * HEAD_DIM
IQ_W = H_IDX * D_IDX
BC_W = 4 * WIDTH_B + 2 * C_CONV

F32 = jnp.float32
BF16 = jnp.bfloat16
I32 = jnp.int32

QB = 128
SCORE_TILES = 2
ATT_TILES = 4
SUB = 16
HALO = 32
FFN_CHUNK = 256
VMEM_LIMIT = 56 * 1024 * 1024
INT_MIN = -2 ** 31
NEG_INF_KEY = INT_MIN + 0x7FFFFF
NEG_BIG = -1e30

NT_DIMS = (((1,), (1,)), ((), ()))
TN_DIMS = (((0,), (0,)), ((), ()))


def _bucket_thresholds():
    half = NUM_BUCKETS // 2
    max_exact = half // 2

    def big(n):
        return min(max_exact + int(math.log(n / max_exact) / math.log(MAX_DISTANCE / max_exact) * (half - max_exact)),
                   half - 1)

    return tuple(min(n for n in range(max_exact, 4 * MAX_DISTANCE) if big(n) >= b) for b in range(max_exact + 1, half))


BUCKET_THRESHOLDS = _bucket_thresholds()


def _params(*sem):
    return pltpu.CompilerParams(dimension_semantics=sem, vmem_limit_bytes=VMEM_LIMIT)


def _resident(shape):
    return pl.BlockSpec(shape, lambda *_: (0,) * len(shape), pipeline_mode=pl.Buffered(1))


def _rms(x, g):
    return x * lax.rsqrt(jnp.mean(x * x, axis=-1, keepdims=True) + EPS) * g


def _sigmoid(x):
    return 1.0 / (1.0 + jnp.exp(-x))


def _ffn_body(x_ref, n_ref, wgu_ref, wd_ref, o_ref, acc_ref):
    x = x_ref[...]
    h = _rms(x, n_ref[0:1, :]).astype(BF16)
    for c in range(D_FF // FFN_CHUNK):
        lo, hi = c * FFN_CHUNK, (c + 1) * FFN_CHUNK
        g = jnp.dot(h, wgu_ref[:, lo:hi], preferred_element_type=F32)
        u = jnp.dot(h, wgu_ref[:, D_FF + lo:D_FF + hi], preferred_element_type=F32)
        a = (g * _sigmoid(g) * u).astype(BF16)
        part = jnp.dot(a, wd_ref[lo:hi, :], preferred_element_type=F32)
        if c == 0:
            acc_ref[...] = part
        else:
            acc_ref[...] += part
    o_ref[...] = x + 0.5 * _rms(acc_ref[...], n_ref[1:2, :])


def _ffn_half(x, norms, w_gu, w_down, tm):
    t = x.shape[0]
    return pl.pallas_call(
        _ffn_body,
        grid=(t // tm,),
        in_specs=[pl.BlockSpec((tm, D_MODEL), lambda i: (i, 0)),
                  _resident((2, D_MODEL)),
                  _resident((D_MODEL, 2 * D_FF)),
                  _resident((D_FF, D_MODEL))],
        out_specs=pl.BlockSpec((tm, D_MODEL), lambda i: (i, 0)),
        out_shape=jax.ShapeDtypeStruct((t, D_MODEL), F32),
        scratch_shapes=[pltpu.VMEM((tm, D_MODEL), F32)],
        compiler_params=_params("arbitrary"),
        name="ffn_half",
    )(x, norms, w_gu, w_down)


def _inproj_body(x_ref, g_ref, wq_ref, wqi_ref, wwi_ref, wk_ref, wv_ref, wki_ref, wbc_ref,
                 qt_ref, qit_ref, wit_ref, k_ref, v_ref, ki_ref, pbc_ref):
    h = _rms(x_ref[...], g_ref[...]).astype(BF16)
    qt = lax.dot_general(wq_ref[...], h, NT_DIMS, preferred_element_type=F32)
    qt_ref[...] = (qt * (HEAD_DIM ** -0.5)).astype(BF16)
    qit_ref[...] = lax.dot_general(wqi_ref[...], h, NT_DIMS, preferred_element_type=F32).astype(BF16)
    wit_ref[...] = lax.dot_general(wwi_ref[...], h, NT_DIMS, preferred_element_type=F32) * (H_IDX ** -0.5)
    k_ref[...] = jnp.dot(h, wk_ref[...], preferred_element_type=F32)
    v_ref[...] = jnp.dot(h, wv_ref[...], preferred_element_type=F32)
    ki_ref[...] = jnp.dot(h, wki_ref[...], preferred_element_type=F32)
    pbc_ref[...] = jnp.dot(h, wbc_ref[...], preferred_element_type=F32)


def _in_proj(x, g, w, tm):
    t = x.shape[0]
    row = lambda i: (i, 0)
    col = lambda i: (0, i)
    return pl.pallas_call(
        _inproj_body,
        grid=(t // tm,),
        in_specs=[pl.BlockSpec((tm, D_MODEL), row),
                  _resident((1, D_MODEL)),
                  _resident((WIDTH_A, D_MODEL)),
                  _resident((IQ_W, D_MODEL)),
                  _resident((H_IDX, D_MODEL)),
                  _resident((D_MODEL, KV_W)),
                  _resident((D_MODEL, KV_W)),
                  _resident((D_MODEL, D_IDX)),
                  _resident((D_MODEL, BC_W))],
        out_specs=[pl.BlockSpec((WIDTH_A, tm), col),
                   pl.BlockSpec((IQ_W, tm), col),
                   pl.BlockSpec((H_IDX, tm), col),
                   pl.BlockSpec((tm, KV_W), row),
                   pl.BlockSpec((tm, KV_W), row),
                   pl.BlockSpec((tm, D_IDX), row),
                   pl.BlockSpec((tm, BC_W), row)],
        out_shape=[jax.ShapeDtypeStruct((WIDTH_A, t), BF16),
                   jax.ShapeDtypeStruct((IQ_W, t), BF16),
                   jax.ShapeDtypeStruct((H_IDX, t), F32),
                   jax.ShapeDtypeStruct((t, KV_W), F32),
                   jax.ShapeDtypeStruct((t, KV_W), F32),
                   jax.ShapeDtypeStruct((t, D_IDX), F32),
                   jax.ShapeDtypeStruct((t, BC_W), F32)],
        compiler_params=_params("arbitrary"),
        name="in_proj",
    )(x, g, w["q_t"], w["qi_t"], w["wi_t"], w["k"], w["v"], w["ki"], w["bc"])


def _bias_body(rb_ref, o_ref):
    j = lax.broadcasted_iota(I32, (QB, QB), 0)
    i = lax.broadcasted_iota(I32, (QB, QB), 1)
    half = NUM_BUCKETS // 2
    max_exact = half // 2
    for var, off in ((0, -QB), (1, 0)):
        rel = j - i + off
        n = jnp.abs(rel)
        big = jnp.full((QB, QB), max_exact, I32)
        for thr in BUCKET_THRESHOLDS:
            big = big + jnp.where(n >= thr, 1, 0)
        bucket = jnp.where(n < max_exact, n, big) + jnp.where(rel > 0, half, 0)
        for head in range(H_A):
            g, hh = divmod(head, GROUP)
            far = rb_ref[half - 1, head]
            tile = jnp.zeros((QB, QB), F32)
            for b in range(NUM_BUCKETS):
                tile = jnp.where(bucket == b, rb_ref[b, head] - far, tile)
            o_ref[g, var, :, hh * QB:(hh + 1) * QB] = tile


def _bias_tiles(rel_bias):
    return pl.pallas_call(
        _bias_body,
        in_specs=[pl.BlockSpec(memory_space=pltpu.SMEM)],
        out_shape=jax.ShapeDtypeStruct((N_KV, 2, QB, GROUP * QB), F32),
        name="rel_bias_tiles",
    )(rel_bias)


def _dsa_body(qt_ref, qit_ref, wit_ref, k_ref, v_ref, ki_ref, bias_ref, o_ref,
              keys_ref, kb_ref, kib_ref, vt_ref, acc_ref, *,
              q_base, tile_base, l_valid, topk, idx_bits, n_tiles_pad):
    qb = pl.program_id(1)
    nt = tile_base + qb + 1
    srows = SCORE_TILES * QB
    arows = ATT_TILES * QB

    @pl.when(qb == 0)
    def _():
        for t in range(n_tiles_pad):
            rows = slice(t * QB, (t + 1) * QB)
            kt = k_ref[rows, :]
            for g in range(N_KV):
                kb_ref[g, rows, :] = kt[:, g * HEAD_DIM:(g + 1) * HEAD_DIM].astype(BF16)
            kib_ref[rows, :] = ki_ref[rows, :].astype(BF16)
            vt_ref[t] = v_ref[rows, :].T.astype(BF16)

    def admissible(r0, rows):
        k_pos = r0 + lax.broadcasted_iota(I32, (rows, QB), 0)
        q_chunk = (q_base + qb * QB + lax.broadcasted_iota(I32, (rows, QB), 1)) >> CHUNK_SHIFT
        return jnp.where(k_pos < l_valid, k_pos >> CHUNK_SHIFT, 2 ** 30) <= q_chunk

    qi_cat = jnp.concatenate([qit_ref[h * D_IDX:(h + 1) * D_IDX, :] for h in range(H_IDX)], axis=1)
    wi_cat = jnp.concatenate([wit_ref[h:h + 1, :] for h in range(H_IDX)], axis=1)
    n2 = (nt + SCORE_TILES - 1) // SCORE_TILES

    def score_step(u, carry):
        r0 = pl.multiple_of(u * srows, srows)
        lg = jnp.dot(kib_ref[pl.ds(r0, srows), :], qi_cat, preferred_element_type=F32)
        w = jnp.maximum(lg, 0.0) * wi_cat
        sc = w[:, 0:QB]
        for h in range(1, H_IDX):
            sc = sc + w[:, h * QB:(h + 1) * QB]
        sc = jnp.where(admissible(r0, srows), sc + 0.0, -jnp.inf)
        bits = pltpu.bitcast(sc, I32)
        keys_ref[pl.ds(r0, srows), :] = bits ^ ((bits >> 31) & 0x7FFFFFFF)
        return carry

    lax.fori_loop(0, n2, score_step, 0)

    def count(hit_fn):
        def body(t, acc):
            r0 = pl.multiple_of(t * srows, srows)
            hit = hit_fn(keys_ref[pl.ds(r0, srows), :], r0)
            return acc + hit.reshape(srows // 8, 8, QB).sum(axis=0)

        acc = lax.fori_loop(0, n2, body, jnp.zeros((8, QB), I32))
        return acc.sum(axis=0, keepdims=True)

    def count_ge(cand):
        return count(lambda blk, r0: jnp.where(blk >= cand, 1, 0))

    c0 = count_ge(jnp.zeros((1, QB), I32))
    thr = jnp.where(c0 >= topk, 0, INT_MIN)
    cnt = jnp.where(c0 >= topk, c0, n2 * srows)

    def bit_step(bi, carry):
        thr, cnt = carry
        cand = thr + lax.shift_left(jnp.int32(1), 30 - bi)
        c = count_ge(cand)
        return jnp.where(c >= topk, cand, thr), jnp.where(c >= topk, c, cnt)

    thr, cnt = lax.fori_loop(0, 31, bit_step, (thr, cnt))

    excess = jnp.where(thr > NEG_INF_KEY, cnt - topk, 0)

    @pl.when(jnp.max(excess) > 0)
    def _():
        need = topk - count_ge(thr + 1)
        rowi = lax.broadcasted_iota(I32, (srows, QB), 0)

        def idx_step(bi, last):
            cand = last + lax.shift_left(jnp.int32(1), idx_bits - 1 - bi)
            c = count(lambda blk, r0: jnp.where(blk == thr, jnp.where(r0 + rowi < cand, 1, 0), 0))
            return jnp.where(c < need, cand, last)

        last = lax.fori_loop(0, idx_bits, idx_step, jnp.zeros((1, QB), I32))

        def demote(t, carry):
            r0 = pl.multiple_of(t * srows, srows)
            blk = keys_ref[pl.ds(r0, srows), :]
            keys_ref[pl.ds(r0, srows), :] = jnp.where(blk == thr, jnp.where(r0 + rowi > last, thr - 1, blk), blk)
            return carry

        lax.fori_loop(0, n2, demote, 0)

    q_cat = [jnp.concatenate([qt_ref[(g * GROUP + hh) * HEAD_DIM:(g * GROUP + hh + 1) * HEAD_DIM, :]
                              for hh in range(GROUP)], axis=1) for g in range(N_KV)]
    acc_ref[...] = jnp.zeros(acc_ref.shape, F32)

    def softmax_update(g, s, vt, m_old, l_old):
        m_new = jnp.maximum(m_old, s.max(axis=0, keepdims=True))
        alpha = jnp.exp(m_old - m_new)
        p = jnp.exp(s - m_new)
        l_new = alpha * l_old + p.sum(axis=0, keepdims=True)
        acc_ref[g] = acc_ref[g] * alpha + jnp.dot(vt, p.astype(BF16), preferred_element_type=F32)
        return m_new, l_new

    far_rows = (nt - 2) * QB
    row_a = lax.broadcasted_iota(I32, (arows, QB), 0)

    def far_step(u, carry):
        r0 = pl.multiple_of(u * arows, arows)
        key = keys_ref[pl.ds(r0, arows), :]
        madd = jnp.where(key >= thr, jnp.where(r0 + row_a < far_rows, 0.0, -jnp.inf), -jnp.inf)
        madd = jnp.concatenate([madd] * GROUP, axis=1)
        out = []
        for g in range(N_KV):
            s = jnp.dot(kb_ref[g, pl.ds(r0, arows), :], q_cat[g], preferred_element_type=F32) + madd
            vt = jnp.concatenate([vt_ref[u * ATT_TILES + i, g * HEAD_DIM:(g + 1) * HEAD_DIM, :]
                                  for i in range(ATT_TILES)], axis=1)
            out += softmax_update(g, s, vt, carry[2 * g], carry[2 * g + 1])
        return tuple(out)

    init = (jnp.full((1, GROUP * QB), NEG_BIG, F32), jnp.zeros((1, GROUP * QB), F32)) * N_KV
    state = lax.fori_loop(0, (nt + 1) // ATT_TILES, far_step, init)

    near = (jnp.maximum(nt - 2, 0), nt - 1)

    def near_mask(tile):
        r0 = pl.multiple_of(tile * QB, QB)
        key = keys_ref[pl.ds(r0, QB), :]
        return jnp.where(key >= thr, jnp.where(admissible(r0, QB), 0.0, -jnp.inf), -jnp.inf)

    madd = jnp.concatenate([jnp.concatenate([jnp.where(nt >= 2, near_mask(near[0]), -jnp.inf)] * GROUP, axis=1),
                            jnp.concatenate([near_mask(near[1])] * GROUP, axis=1)], axis=0)
    fin = []
    for g in range(N_KV):
        kk = jnp.concatenate([kb_ref[g, pl.ds(pl.multiple_of(tl * QB, QB), QB), :] for tl in near], axis=0)
        bias = jnp.concatenate([bias_ref[g, 0], bias_ref[g, 1]], axis=0)
        s = jnp.dot(kk, q_cat[g], preferred_element_type=F32) + (bias + madd)
        vt = jnp.concatenate([vt_ref[tl, g * HEAD_DIM:(g + 1) * HEAD_DIM, :] for tl in near], axis=1)
        fin.append(softmax_update(g, s, vt, state[2 * g], state[2 * g + 1]))

    heads = []
    for g in range(N_KV):
        og = acc_ref[g] / fin[g][1]
        heads += [og[:, hh * QB:(hh + 1) * QB] for hh in range(GROUP)]
    o_ref[...] = jnp.concatenate(heads, axis=0).T


def _dsa_key_rows(n_qb, tile_base):
    return -(-(tile_base + n_qb) // SCORE_TILES) * SCORE_TILES * QB


def _dsa(qt, qit, wit, k, v, ki, bias, *, n_qb, q_base, tile_base, l_valid):
    b, lk, _ = k.shape
    assert lk == _dsa_key_rows(n_qb, tile_base)
    n_tiles_pad = lk // QB
    topk = min(TOPK_MAX, l_valid // 4)
    body = functools.partial(_dsa_body, q_base=q_base, tile_base=tile_base, l_valid=l_valid, topk=topk,
                             idx_bits=max(1, (lk - 1).bit_length()), n_tiles_pad=n_tiles_pad)
    tok = lambda bi, qi: (0, bi * n_qb + qi)
    per_batch = lambda bi, qi: (bi, 0, 0)
    return pl.pallas_call(
        body,
        grid=(b, n_qb),
        in_specs=[pl.BlockSpec((WIDTH_A, QB), tok),
                  pl.BlockSpec((IQ_W, QB), tok),
                  pl.BlockSpec((H_IDX, QB), tok),
                  pl.BlockSpec((None, lk, KV_W), per_batch),
                  pl.BlockSpec((None, lk, KV_W), per_batch),
                  pl.BlockSpec((None, lk, D_IDX), per_batch),
                  _resident((N_KV, 2, QB, GROUP * QB))],
        out_specs=pl.BlockSpec((QB, WIDTH_A), lambda bi, qi: (bi * n_qb + qi, 0)),
        out_shape=jax.ShapeDtypeStruct((b * n_qb * QB, WIDTH_A), F32),
        scratch_shapes=[pltpu.VMEM((lk, QB), I32),
                        pltpu.VMEM((N_KV, lk, HEAD_DIM), BF16),
                        pltpu.VMEM((lk, D_IDX), BF16),
                        pltpu.VMEM((n_tiles_pad, KV_W, QB), BF16),
                        pltpu.VMEM((N_KV, HEAD_DIM, GROUP * QB), F32)],
        compiler_params=_params("arbitrary", "arbitrary"),
        name="dsa_attention",
    )(qt, qit, wit, k, v, ki, bias)


def _hgrn_conv_body(pbc_ref, lbl_ref, gn_ref, cw_ref, cb_ref, lng_ref, lnb_ref, s0_ref, cbuf_ref,
                    obc_ref, sout_ref, cout_ref,
                    st_ref, ubuf_ref, xbuf_ref, r_ref, a_ref, qs_ref, kk_ref, qd_ref, kd_ref, dd_ref, oh_ref, *,
                    layer, tm):
    t = pl.program_id(1)
    n_sub = tm // SUB
    hw = H_B * DK_B
    ri = lax.broadcasted_iota(I32, (hw, hw), 0)
    ci = lax.broadcasted_iota(I32, (hw, hw), 1)
    same_head = (ri // DK_B) == (ci // DK_B)
    bd = jnp.where(same_head, 1.0, 0.0)

    @pl.when(t == 0)
    def _():
        s_bd = jnp.concatenate([s0_ref[...]] * H_B, axis=1) * bd
        st_ref[...] = s_bd.T
        ubuf_ref[0:HALO - (CONV_W - 1), :] = jnp.zeros((HALO - (CONV_W - 1), C_CONV), F32)
        ubuf_ref[HALO - (CONV_W - 1):HALO, :] = cbuf_ref[...]

    @pl.when(t > 0)
    def _():
        ubuf_ref[0:HALO, :] = ubuf_ref[tm:tm + HALO, :]

    lg = lbl_ref[...]
    e = jnp.exp(lg - lg.max(axis=0, keepdims=True))
    p = e / e.sum(axis=0, keepdims=True)
    lb = jnp.zeros((1, hw), F32)
    for m in range(1, layer + 1):
        lb = lb + p[m:m + 1, :]

    qraw = pbc_ref[:, 0:hw]
    z = pbc_ref[:, hw:2 * hw]
    qs_ref[...] = qraw * _sigmoid(qraw)
    log_sig = -(jnp.maximum(-z, 0.0) + jnp.log1p(jnp.exp(-jnp.abs(z))))
    a1 = jnp.log(lb)
    b1 = jnp.log1p(-lb) + log_sig
    log_f = jnp.maximum(a1, b1) + jnp.log1p(jnp.exp(-jnp.abs(a1 - b1)))
    kk_ref[...] = (1.0 - lb) * _sigmoid(-z)

    tr = lax.broadcasted_iota(I32, (tm, tm), 0)
    tc = lax.broadcasted_iota(I32, (tm, tm), 1)
    same_sub = (tr // SUB) == (tc // SUB)
    tri = jnp.where(same_sub, jnp.where(tc <= tr, 1.0, 0.0), 0.0)
    blk = jnp.where(same_sub, 1.0, 0.0)
    a_loc = jnp.dot(tri, log_f, preferred_element_type=F32, precision=lax.Precision.HIGHEST)
    a_end = jnp.dot(blk, log_f, preferred_element_type=F32, precision=lax.Precision.HIGHEST)
    a_ref[...] = a_loc
    qd_ref[...] = (qs_ref[...] * jnp.exp(a_loc)).astype(BF16)
    kd_ref[...] = (kk_ref[...] * jnp.exp(a_end - a_loc)).astype(BF16)
    dd_ref[...] = jnp.exp(a_end)

    ii = lax.broadcasted_iota(I32, (SUB, hw), 0)

    def build_x(s, carry):
        r0 = pl.multiple_of(s * SUB, SUB)
        a_s = a_ref[pl.ds(r0, SUB), :]
        q_s = qs_ref[pl.ds(r0, SUB), :]
        for j in range(SUB):
            a_j = a_ref[pl.ds(r0 + j, 1), :]
            k_j = kk_ref[pl.ds(r0 + j, 1), :]
            dec = jnp.exp(jnp.where(ii >= j, a_s - a_j, -jnp.inf))
            x0 = pl.multiple_of(s * SUB * SUB + j * SUB, SUB)
            xbuf_ref[pl.ds(x0, SUB), :] = (q_s * k_j * dec).astype(BF16)
        return carry

    lax.fori_loop(0, n_sub, build_x, 0)
    r_ref[...] = jnp.dot(xbuf_ref[...], bd.astype(BF16), preferred_element_type=F32)

    def sub_step(s, carry):
        r0 = pl.multiple_of(s * SUB, SUB)
        o = jnp.zeros((SUB, hw), F32)
        for j in range(SUB):
            x0 = pl.multiple_of(s * SUB * SUB + j * SUB, SUB)
            o = o + r_ref[pl.ds(x0, SUB), :] * pbc_ref[pl.ds(r0 + j, 1), 2 * hw:3 * hw]
        st = st_ref[...]
        o = o + lax.dot_general(qd_ref[pl.ds(r0, SUB), :], st.astype(BF16), NT_DIMS, preferred_element_type=F32)
        oh_ref[pl.ds(r0, SUB), :] = o
        val = pbc_ref[pl.ds(r0, SUB), 2 * hw:3 * hw].astype(BF16)
        ut = lax.dot_general(val, kd_ref[pl.ds(r0, SUB), :], TN_DIMS, preferred_element_type=F32)
        st_ref[...] = st * dd_ref[pl.ds(r0, 1), :] + ut * bd
        return carry

    lax.fori_loop(0, n_sub, sub_step, 0)

    o = oh_ref[...]
    ms = jnp.dot(o * o, bd, preferred_element_type=F32, precision=lax.Precision.HIGHEST) * (1.0 / DV_B)
    gate = pbc_ref[:, 3 * hw:4 * hw]
    obc_ref[:, 0:WIDTH_B] = o * lax.rsqrt(ms + EPS) * gn_ref[...] * (gate * _sigmoid(gate))

    c0 = 4 * hw
    u = pbc_ref[:, c0:c0 + C_CONV] * _sigmoid(pbc_ref[:, c0 + C_CONV:c0 + 2 * C_CONV])
    ubuf_ref[HALO:HALO + tm, :] = u
    y = jnp.zeros((tm, C_CONV), F32)
    for w in range(CONV_W):
        y = y + ubuf_ref[pl.ds(HALO - (CONV_W - 1) + w, tm), :] * cw_ref[w:w + 1, :]
    y = y + cb_ref[...]
    yc = y - jnp.mean(y, axis=-1, keepdims=True)
    ln = yc * lax.rsqrt(jnp.mean(yc * yc, axis=-1, keepdims=True) + EPS) * lng_ref[...] + lnb_ref[...]
    obc_ref[:, WIDTH_B:WIDTH_BC] = ln * _sigmoid(ln)

    @pl.when(t == pl.num_programs(1) - 1)
    def _():
        s_bd = st_ref[...].T
        s_new = s_bd[:, 0:DV_B]
        for h in range(1, H_B):
            s_new = s_new + s_bd[:, h * DV_B:(h + 1) * DV_B]
        sout_ref[...] = s_new
        cout_ref[...] = ubuf_ref[tm + HALO - (CONV_W - 1):tm + HALO, :]


def _hgrn_conv(pbc, lb_logits, gn, cw, cb, lng, lnb, s0, cbuf, *, layer, seq, tm):
    b = s0.shape[0]
    n_t = seq // tm
    n_sub = tm // SUB
    hw = H_B * DK_B
    body = functools.partial(_hgrn_conv_body, layer=layer, tm=tm)
    per_batch = lambda bi, ti: (bi, 0, 0)
    tok = lambda bi, ti: (bi * n_t + ti, 0)
    return pl.pallas_call(
        body,
        grid=(b, n_t),
        in_specs=[pl.BlockSpec((tm, BC_W), tok),
                  _resident((DEPTH, hw)),
                  _resident((1, hw)),
                  _resident((CONV_W, C_CONV)),
                  _resident((1, C_CONV)),
                  _resident((1, C_CONV)),
                  _resident((1, C_CONV)),
                  pl.BlockSpec((None, hw, DV_B), per_batch),
                  pl.BlockSpec((None, CONV_W - 1, C_CONV), per_batch)],
        out_specs=[pl.BlockSpec((tm, WIDTH_BC), tok),
                   pl.BlockSpec((None, hw, DV_B), per_batch),
                   pl.BlockSpec((None, CONV_W - 1, C_CONV), per_batch)],
        out_shape=[jax.ShapeDtypeStruct((b * seq, WIDTH_BC), F32),
                   jax.ShapeDtypeStruct((b, hw, DV_B), F32),
                   jax.ShapeDtypeStruct((b, CONV_W - 1, C_CONV), F32)],
        scratch_shapes=[pltpu.VMEM((hw, hw), F32),
                        pltpu.VMEM((tm + HALO, C_CONV), F32),
                        pltpu.VMEM((n_sub * SUB * SUB, hw), BF16),
                        pltpu.VMEM((n_sub * SUB * SUB, hw), F32),
                        pltpu.VMEM((tm, hw), F32),
                        pltpu.VMEM((tm, hw), F32),
                        pltpu.VMEM((tm, hw), F32),
                        pltpu.VMEM((tm, hw), BF16),
                        pltpu.VMEM((tm, hw), BF16),
                        pltpu.VMEM((tm, hw), F32),
                        pltpu.VMEM((tm, hw), F32)],
        compiler_params=_params("arbitrary", "arbitrary"),
        name="hgrn_conv",
    )(pbc, lb_logits, gn, cw, cb, lng, lnb, s0, cbuf)


def _outproj_body(x_ref, oa_ref, obc_ref, wa_ref, wbc_ref, g_ref, o_ref):
    m = jnp.dot(oa_ref[...].astype(BF16), wa_ref[...], preferred_element_type=F32)
    m = m + jnp.dot(obc_ref[...].astype(BF16), wbc_ref[...], preferred_element_type=F32)
    o_ref[...] = x_ref[...] + _rms(m, g_ref[...])


def _out_proj(x, oa, obc, wa, wbc, g, tm):
    t = x.shape[0]
    row = lambda i: (i, 0)
    return pl.pallas_call(
        _outproj_body,
        grid=(t // tm,),
        in_specs=[pl.BlockSpec((tm, D_MODEL), row),
                  pl.BlockSpec((tm, WIDTH_A), row),
                  pl.BlockSpec((tm, WIDTH_BC), row),
                  _resident((WIDTH_A, D_MODEL)),
                  _resident((WIDTH_BC, D_MODEL)),
                  _resident((1, D_MODEL))],
        out_specs=pl.BlockSpec((tm, D_MODEL), row),
        out_shape=jax.ShapeDtypeStruct((t, D_MODEL), F32),
        compiler_params=_params("arbitrary"),
        name="out_proj",
    )(x, oa, obc, wa, wbc, g)


def _split_w_in(w_in):
    sizes = (WIDTH_A, KV_W, KV_W, IQ_W, D_IDX, H_IDX)
    offs = [0]
    for s in sizes:
        offs.append(offs[-1] + s)
    wb = w_in.astype(BF16)
    q, k, v, qi, ki, wi = (wb[:, offs[n]:offs[n + 1]] for n in range(len(sizes)))
    return {"q_t": q.T, "qi_t": qi.T, "wi_t": wi.T, "k": k, "v": v, "ki": ki, "bc": wb[:, offs[-1]:]}


def _layer(x, past, lw, bias, *, layer, batch, seq, tm_tok, tm_seq):
    x = _ffn_half(x, lw["norm_ffn1"], lw["ffn1_gu"], lw["ffn1_down"], tm_tok)
    qt, qit, wit, k_new, v_new, ki_new, pbc = _in_proj(x, lw["norm_mix"][0:1], lw["w_in"], tm_tok)
    hw = H_B * DK_B
    if past is None:
        assert seq == _dsa_key_rows(seq // QB, 0)
        o_a = _dsa(qt, qit, wit, k_new.reshape(batch, seq, KV_W), v_new.reshape(batch, seq, KV_W),
                   ki_new.reshape(batch, seq, D_IDX), bias, n_qb=seq // QB, q_base=0, tile_base=0, l_valid=seq)
        s0 = jnp.zeros((batch, hw, DV_B), F32)
        cbuf = jnp.zeros((batch, CONV_W - 1, C_CONV), F32)
    else:
        k_c, v_c, ki_c, s0, cbuf = past
        past_len = k_c.shape[1]
        assert past_len % QB == 0 and seq <= QB
        l_valid = past_len + seq
        rows = _dsa_key_rows(1, past_len // QB)
        pad_k = lambda c, n, w: jnp.pad(jnp.concatenate([c.reshape(batch, past_len, w), n.reshape(batch, seq, w)], axis=1),
                                        ((0, 0), (0, rows - l_valid), (0, 0)))
        pad_q = lambda a: jnp.pad(a.reshape(a.shape[0], batch, seq), ((0, 0), (0, 0), (0, QB - seq))).reshape(a.shape[0], batch * QB)
        o_a = _dsa(pad_q(qt), pad_q(qit), pad_q(wit), pad_k(k_c, k_new, KV_W), pad_k(v_c, v_new, KV_W),
                   pad_k(ki_c, ki_new, D_IDX), bias, n_qb=1, q_base=past_len, tile_base=past_len // QB, l_valid=l_valid)
        o_a = o_a.reshape(batch, QB, WIDTH_A)[:, :seq].reshape(batch * seq, WIDTH_A)
        s0 = s0.reshape(batch, hw, DV_B)
    obc, s_new, c_new = _hgrn_conv(pbc, lw["lb_logits"], lw["gnorm"], lw["conv_w"], lw["conv_b"], lw["ln_g"], lw["ln_b"],
                                   s0, cbuf, layer=layer, seq=seq, tm=tm_seq)
    x = _out_proj(x, o_a, obc, lw["w_out_a"], lw["w_out_bc"], lw["norm_mix"][1:2], tm_tok)
    x = _ffn_half(x, lw["norm_ffn2"], lw["ffn2_gu"], lw["ffn2_down"], tm_tok)
    state = (k_new.reshape(batch, seq, N_KV, HEAD_DIM), v_new.reshape(batch, seq, N_KV, HEAD_DIM),
             ki_new.reshape(batch, seq, D_IDX), s_new.reshape(batch, H_B, DK_B, DV_B), c_new)
    return x, state


def kernel(x_prompt, x_sample, cache_attn_k, cache_attn_v, cache_idx_k, state_hgrn, state_conv, norm_ffn1, ffn1_w_gate_up, ffn1_w_down, norm_mix, w_in, w_out, rel_bias, hgrn_lb_logits, hgrn_gnorm, conv_w, conv_b, conv_ln_g, conv_ln_b, norm_ffn2, ffn2_w_gate_up, ffn2_w_down):
    bp, sp, _ = x_prompt.shape
    bs, ss, _ = x_sample.shape
    bias = _bias_tiles(rel_bias)
    xp = x_prompt.reshape(bp * sp, D_MODEL)
    xs = x_sample.reshape(bs * ss, D_MODEL)
    st_p, st_s = [], []
    for l in range(DEPTH):
        lw = {
            "norm_ffn1": norm_ffn1[l], "ffn1_gu": ffn1_w_gate_up[l].astype(BF16), "ffn1_down": ffn1_w_down[l].astype(BF16),
            "norm_mix": norm_mix[l], "w_in": _split_w_in(w_in[l]),
            "w_out_a": w_out[l, :WIDTH_A].astype(BF16), "w_out_bc": w_out[l, WIDTH_A:].astype(BF16),
            "lb_logits": hgrn_lb_logits, "gnorm": jnp.tile(hgrn_gnorm[l], H_B)[None, :],
            "conv_w": conv_w[l], "conv_b": conv_b[l][None, :], "ln_g": conv_ln_g[l][None, :], "ln_b": conv_ln_b[l][None, :],
            "norm_ffn2": norm_ffn2[l], "ffn2_gu": ffn2_w_gate_up[l].astype(BF16), "ffn2_down": ffn2_w_down[l].astype(BF16),
        }
        xp, sp_state = _layer(xp, None, lw, bias, layer=l, batch=bp, seq=sp, tm_tok=min(512, bp * sp), tm_seq=min(256, sp))
        past = (cache_attn_k[l], cache_attn_v[l], cache_idx_k[l], state_hgrn[l], state_conv[l])
        xs, ss_state = _layer(xs, past, lw, bias, layer=l, batch=bs, seq=ss, tm_tok=min(512, bs * ss), tm_seq=min(256, ss))
        st_p.append(sp_state)
        st_s.append(ss_state)
    stack = lambda sts, j: jnp.stack([s[j] for s in sts], axis=0)
    return (xp.reshape(bp, sp, D_MODEL), xs.reshape(bs, ss, D_MODEL),
            *(stack(st_p, j) for j in range(5)), *(stack(st_s, j) for j in range(5)))
```

```python
import functools
import math

import jax
import jax.numpy as jnp
from jax import lax
from jax.experimental import pallas as pl
from jax.experimental.pallas import tpu as pltpu

D_MODEL = 1024
DEPTH = 2
CHUNK = 64
CHUNK_SHIFT = 6
H_A = 8
HEAD_DIM = 64
N_KV = 2
GROUP = H_A // N_KV
H_IDX = 8
D_IDX = 64
TOPK_MAX = 256
NUM_BUCKETS = 32
MAX_DISTANCE = 128
H_B = 4
DK_B = 64
DV_B = 64
C_CONV = 256
CONV_W = 31
D_FF = 2816
EPS = 1e-6

WIDTH_A = H_A * HEAD_DIM
WIDTH_B = H_B * DV_B
WIDTH_BC = WIDTH_B + C_CONV
KV_W = N_KV * HEAD_DIM
IQ_W = H_IDX * D_IDX
BC_W = 4 * WIDTH_B + 2 * C_CONV

F32 = jnp.float32
BF16 = jnp.bfloat16
I32 = jnp.int32
I16 = jnp.int16

LANES = 128
SUBLANES = 8
PACK16 = 2 * SUBLANES
QB = LANES
SCORE_TILES = 2
ATT_TILES = 4
VT_ROWS = HEAD_DIM + PACK16
SUB = 16
HALO = 32
FFN_CHUNK = 256
VMEM_LIMIT = 56 * 1024 * 1024
INT_MIN = -2 ** 31
NEG_INF_KEY = INT_MIN + 0x7FFFFF
I16_MIN = -2 ** 15
I16_MAX = 2 ** 15 - 1
NEG_BIG = -1e30
MASK_NEG = -2.0 ** 100
LOG2E = 1.4426950408889634

NT_DIMS = (((1,), (1,)), ((), ()))
TN_DIMS = (((0,), (0,)), ((), ()))


def _bucket_thresholds():
    half = NUM_BUCKETS // 2
    max_exact = half // 2

    def big(n):
        return min(max_exact + int(math.log(n / max_exact) / math.log(MAX_DISTANCE / max_exact) * (half - max_exact)),
                   half - 1)

    return tuple(min(n for n in range(max_exact, 4 * MAX_DISTANCE) if big(n) >= b) for b in range(max_exact + 1, half))


BUCKET_THRESHOLDS = _bucket_thresholds()


def _params(*sem):
    return pltpu.CompilerParams(dimension_semantics=sem, vmem_limit_bytes=VMEM_LIMIT)


def _resident(shape):
    return pl.BlockSpec(shape, lambda *_: (0,) * len(shape), pipeline_mode=pl.Buffered(1))


def _rms(x, g):
    return x * lax.rsqrt(jnp.mean(x * x, axis=-1, keepdims=True) + EPS) * g


def _sigmoid(x):
    return 0.5 * jnp.tanh(0.5 * x) + 0.5


def _sum_leading(x):
    parts = [x[i] for i in range(x.shape[0])]
    while len(parts) > 1:
        parts = [a + b for a, b in zip(parts[0::2], parts[1::2])] + ([parts[-1]] if len(parts) % 2 else [])
    return parts[0]


def _ffn_body(x_ref, n_ref, wgu_ref, wd_ref, o_ref, acc_ref):
    x = x_ref[...]
    h = _rms(x, n_ref[0:1, :]).astype(BF16)
    for c in range(D_FF // FFN_CHUNK):
        lo, hi = c * FFN_CHUNK, (c + 1) * FFN_CHUNK
        g = jnp.dot(h, wgu_ref[:, lo:hi], preferred_element_type=F32)
        u = jnp.dot(h, wgu_ref[:, D_FF + lo:D_FF + hi], preferred_element_type=F32)
        a = (g * _sigmoid(g) * u).astype(BF16)
        part = jnp.dot(a, wd_ref[lo:hi, :], preferred_element_type=F32)
        if c == 0:
            acc_ref[...] = part
        else:
            acc_ref[...] += part
    o_ref[...] = x + 0.5 * _rms(acc_ref[...], n_ref[1:2, :])


def _ffn_half(x, norms, w_gu, w_down, tm):
    t = x.shape[0]
    return pl.pallas_call(
        _ffn_body,
        grid=(t // tm,),
        in_specs=[pl.BlockSpec((tm, D_MODEL), lambda i: (i, 0)),
                  _resident((2, D_MODEL)),
                  _resident((D_MODEL, 2 * D_FF)),
                  _resident((D_FF, D_MODEL))],
        out_specs=pl.BlockSpec((tm, D_MODEL), lambda i: (i, 0)),
        out_shape=jax.ShapeDtypeStruct((t, D_MODEL), F32),
        scratch_shapes=[pltpu.VMEM((tm, D_MODEL), F32)],
        compiler_params=_params("arbitrary"),
        name="ffn_half",
    )(x, norms, w_gu, w_down)


def _inproj_body(x_ref, g_ref, wq_ref, wqi_ref, wwi_ref, wk_ref, wv_ref, wki_ref, wbc_ref,
                 qt_ref, qit_ref, wit_ref, k_ref, v_ref, ki_ref, pbc_ref):
    h = _rms(x_ref[...], g_ref[...]).astype(BF16)
    qt = lax.dot_general(wq_ref[...], h, NT_DIMS, preferred_element_type=F32)
    qt_ref[...] = (qt * (HEAD_DIM ** -0.5 * LOG2E)).astype(BF16)
    qit_ref[...] = lax.dot_general(wqi_ref[...], h, NT_DIMS, preferred_element_type=F32).astype(BF16)
    wit_ref[...] = lax.dot_general(wwi_ref[...], h, NT_DIMS, preferred_element_type=F32) * (H_IDX ** -0.5)
    k_ref[...] = jnp.dot(h, wk_ref[...], preferred_element_type=F32)
    v_ref[...] = jnp.dot(h, wv_ref[...], preferred_element_type=F32)
    ki_ref[...] = jnp.dot(h, wki_ref[...], preferred_element_type=F32)
    pbc_ref[...] = jnp.dot(h, wbc_ref[...], preferred_element_type=F32)


def _in_proj(x, g, w, tm):
    t = x.shape[0]
    row = lambda i: (i, 0)
    col = lambda i: (0, i)
    return pl.pallas_call(
        _inproj_body,
        grid=(t // tm,),
        in_specs=[pl.BlockSpec((tm, D_MODEL), row),
                  _resident((1, D_MODEL)),
                  _resident((WIDTH_A, D_MODEL)),
                  _resident((IQ_W, D_MODEL)),
                  _resident((H_IDX, D_MODEL)),
                  _resident((D_MODEL, KV_W)),
                  _resident((D_MODEL, KV_W)),
                  _resident((D_MODEL, D_IDX)),
                  _resident((D_MODEL, BC_W))],
        out_specs=[pl.BlockSpec((WIDTH_A, tm), col),
                   pl.BlockSpec((IQ_W, tm), col),
                   pl.BlockSpec((H_IDX, tm), col),
                   pl.BlockSpec((tm, KV_W), row),
                   pl.BlockSpec((tm, KV_W), row),
                   pl.BlockSpec((tm, D_IDX), row),
                   pl.BlockSpec((tm, BC_W), row)],
        out_shape=[jax.ShapeDtypeStruct((WIDTH_A, t), BF16),
                   jax.ShapeDtypeStruct((IQ_W, t), BF16),
                   jax.ShapeDtypeStruct((H_IDX, t), F32),
                   jax.ShapeDtypeStruct((t, KV_W), F32),
                   jax.ShapeDtypeStruct((t, KV_W), F32),
                   jax.ShapeDtypeStruct((t, D_IDX), F32),
                   jax.ShapeDtypeStruct((t, BC_W), F32)],
        compiler_params=_params("arbitrary"),
        name="in_proj",
    )(x, g, w["q_t"], w["qi_t"], w["wi_t"], w["k"], w["v"], w["ki"], w["bc"])


def _bias_body(rb_ref, o_ref):
    j = lax.broadcasted_iota(I32, (QB, QB), 0)
    i = lax.broadcasted_iota(I32, (QB, QB), 1)
    half = NUM_BUCKETS // 2
    max_exact = half // 2
    for var, off in ((0, -QB), (1, 0)):
        rel = j - i + off
        n = jnp.abs(rel)
        big = jnp.full((QB, QB), max_exact, I32)
        for thr in BUCKET_THRESHOLDS:
            big = big + jnp.where(n >= thr, 1, 0)
        bucket = jnp.where(n < max_exact, n, big) + jnp.where(rel > 0, half, 0)
        for head in range(H_A):
            g, hh = divmod(head, GROUP)
            far = rb_ref[half - 1, head]
            tile = jnp.zeros((QB, QB), F32)
            for b in range(NUM_BUCKETS):
                tile = jnp.where(bucket == b, (rb_ref[b, head] - far) * LOG2E, tile)
            o_ref[g, var, :, hh * QB:(hh + 1) * QB] = tile


def _bias_tiles(rel_bias):
    return pl.pallas_call(
        _bias_body,
        in_specs=[pl.BlockSpec(memory_space=pltpu.SMEM)],
        out_shape=jax.ShapeDtypeStruct((N_KV, 2, QB, GROUP * QB), F32),
        name="rel_bias_tiles",
    )(rel_bias)


def _dsa_body(qt_ref, qit_ref, wit_ref, k_ref, v_ref, ki_ref, bias_ref, o_ref,
              keys_ref, khi_ref, klo_ref, kb_ref, kib_ref, vt_ref, acc_ref, *,
              q_base, tile_base, l_valid, topk, idx_bits, n_tiles_pad):
    qb = pl.program_id(1)
    nt = tile_base + qb + 1
    srows = SCORE_TILES * QB
    arows = ATT_TILES * QB

    @pl.when(qb == 0)
    def _():
        ones_rows = jnp.where(lax.broadcasted_iota(I32, (PACK16, QB), 0) == 0, 1.0, 0.0)
        for t in range(n_tiles_pad):
            rows = slice(t * QB, (t + 1) * QB)
            kt = k_ref[rows, :]
            vt = v_ref[rows, :].T
            for g in range(N_KV):
                kg = jnp.concatenate([kt[:, g * HEAD_DIM:(g + 1) * HEAD_DIM], jnp.zeros((QB, LANES - HEAD_DIM), F32)], axis=1)
                kb_ref[g, rows, :] = kg.astype(BF16)
                vt_ref[t, g] = jnp.concatenate([vt[g * HEAD_DIM:(g + 1) * HEAD_DIM, :], ones_rows], axis=0).astype(BF16)
            kib_ref[rows, :] = ki_ref[rows, :].astype(BF16)

    def admissible(r0, rows):
        k_pos = r0 + lax.broadcasted_iota(I32, (rows, QB), 0)
        q_chunk = (q_base + qb * QB + lax.broadcasted_iota(I32, (rows, QB), 1)) >> CHUNK_SHIFT
        return jnp.where(k_pos < l_valid, k_pos >> CHUNK_SHIFT, 2 ** 30) <= q_chunk

    qi_cat = jnp.concatenate([qit_ref[h * D_IDX:(h + 1) * D_IDX, :] for h in range(H_IDX)], axis=1)
    wi_cat = jnp.concatenate([wit_ref[h:h + 1, :] for h in range(H_IDX)], axis=1)
    n2 = (nt + SCORE_TILES - 1) // SCORE_TILES

    def score_step(u, carry):
        r0 = pl.multiple_of(u * srows, srows)
        lg = jnp.dot(kib_ref[pl.ds(r0, srows), :], qi_cat, preferred_element_type=F32)
        w = jnp.maximum(lg, 0.0) * wi_cat
        sc = w[:, 0:QB]
        for h in range(1, H_IDX):
            sc = sc + w[:, h * QB:(h + 1) * QB]
        sc = jnp.where(admissible(r0, srows), sc + 0.0, -jnp.inf)
        bits = pltpu.bitcast(sc, I32)
        key = bits ^ ((bits >> 31) & 0x7FFFFFFF)
        keys_ref[pl.ds(r0, srows), :] = key
        khi_ref[pl.ds(r0, srows), :] = (key >> 16).astype(I16)
        klo_ref[pl.ds(r0, srows), :] = ((key & 0xFFFF) + I16_MIN).astype(I16)
        return carry

    lax.fori_loop(0, n2, score_step, 0)

    def count32(hit_fn):
        def body(t, acc):
            r0 = pl.multiple_of(t * srows, srows)
            hit = hit_fn(keys_ref[pl.ds(r0, srows), :], r0)
            return acc + hit.reshape(srows // SUBLANES, SUBLANES, QB).sum(axis=0)

        acc = lax.fori_loop(0, n2, body, jnp.zeros((SUBLANES, QB), I32))
        return acc.sum(axis=0, keepdims=True)

    def count16(ref, hit_fn):
        def body(t, acc):
            r0 = pl.multiple_of(t * srows, srows)
            hit = hit_fn(ref[pl.ds(r0, srows), :].reshape(srows // PACK16, PACK16, QB))
            return acc + _sum_leading(hit)

        acc = lax.fori_loop(0, n2, body, jnp.zeros((PACK16, QB), I16))
        return acc.astype(I32).sum(axis=0, keepdims=True)

    def bcast16(x):
        return jnp.broadcast_to(x, (PACK16, QB)).astype(I16)[None]

    one16 = jnp.ones((1, PACK16, QB), I16)
    zero16 = jnp.zeros((1, PACK16, QB), I16)

    def count16_ge(ref, cand):
        c16 = bcast16(cand)
        return count16(ref, lambda blk: jnp.where(blk >= c16, one16, zero16))

    def bisect16(count_ge, total):
        c0, need0 = count_ge(jnp.zeros((1, QB), I32))
        val = jnp.where(c0 >= need0, 0, I16_MIN)
        cnt = jnp.where(c0 >= need0, c0, total)

        def step(bi, carry):
            val, cnt = carry
            cand = val + lax.shift_left(jnp.int32(1), 14 - bi)
            c, need = count_ge(cand)
            return jnp.where(c >= need, cand, val), jnp.where(c >= need, c, cnt)

        return lax.fori_loop(0, 15, step, (val, cnt))

    t_hi, _ = bisect16(lambda cand: (count16_ge(khi_ref, cand), topk), n2 * srows)
    above = jnp.where(t_hi == I16_MAX, 0, count16_ge(khi_ref, jnp.minimum(t_hi + 1, I16_MAX)))
    hi16 = bcast16(t_hi)
    low_min = jnp.full((1, PACK16, QB), I16_MIN, I16)

    def mask_low(t, acc):
        r0 = pl.multiple_of(t * srows, srows)
        same = khi_ref[pl.ds(r0, srows), :].reshape(srows // PACK16, PACK16, QB) == hi16
        lo = klo_ref[pl.ds(r0, srows), :].reshape(srows // PACK16, PACK16, QB)
        klo_ref[pl.ds(r0, srows), :] = jnp.where(same, lo, low_min).reshape(srows, QB)
        return acc + _sum_leading(jnp.where(same, one16, zero16))

    same_hi = lax.fori_loop(0, n2, mask_low, jnp.zeros((PACK16, QB), I16)).astype(I32).sum(axis=0, keepdims=True)
    need_lo = topk - above
    t_lo, cnt_lo = bisect16(lambda cand: (count16_ge(klo_ref, cand), need_lo), same_hi)
    thr = lax.shift_left(t_hi, 16) | (t_lo - I16_MIN)
    cnt = above + cnt_lo

    excess = jnp.where(thr > NEG_INF_KEY, cnt - topk, 0)

    @pl.when(jnp.max(excess) > 0)
    def _():
        need = topk - count32(lambda blk, r0: jnp.where(blk > thr, 1, 0))
        rowi = lax.broadcasted_iota(I32, (srows, QB), 0)

        def idx_step(bi, last):
            cand = last + lax.shift_left(jnp.int32(1), idx_bits - 1 - bi)
            c = count32(lambda blk, r0: jnp.where(blk == thr, jnp.where(r0 + rowi < cand, 1, 0), 0))
            return jnp.where(c < need, cand, last)

        last = lax.fori_loop(0, idx_bits, idx_step, jnp.zeros((1, QB), I32))

        def demote(t, carry):
            r0 = pl.multiple_of(t * srows, srows)
            blk = keys_ref[pl.ds(r0, srows), :]
            keys_ref[pl.ds(r0, srows), :] = jnp.where(blk == thr, jnp.where(r0 + rowi > last, thr - 1, blk), blk)
            return carry

        lax.fori_loop(0, n2, demote, 0)

    eye = jnp.where(lax.broadcasted_iota(I32, (QB, QB), 0) == lax.broadcasted_iota(I32, (QB, QB), 1), 1.0, 0.0)
    eye = jnp.concatenate([eye.astype(BF16)] * GROUP, axis=1)
    q_aug = []
    for g in range(N_KV):
        q_cat = jnp.concatenate([qt_ref[(g * GROUP + hh) * HEAD_DIM:(g * GROUP + hh + 1) * HEAD_DIM, :]
                                 for hh in range(GROUP)], axis=1)
        q_aug.append(jnp.concatenate([eye, q_cat, jnp.zeros((LANES - HEAD_DIM, GROUP * QB), BF16)], axis=0))
    acc_ref[...] = jnp.zeros(acc_ref.shape, F32)

    def accumulate(g, p, vt, m_old, m_new):
        alpha = jnp.exp2(m_old - m_new)
        acc_ref[g] = acc_ref[g] * alpha + jnp.dot(vt, p.astype(BF16), preferred_element_type=F32)

    far_rows = (nt - 2) * QB
    row_a = lax.broadcasted_iota(I32, (arows, QB), 0)

    def far_step(u, carry):
        r0 = pl.multiple_of(u * arows, arows)
        key = keys_ref[pl.ds(r0, arows), :]
        pick = jnp.where(key >= thr, jnp.where(r0 + row_a < far_rows, 0.0, MASK_NEG), MASK_NEG).astype(BF16)
        out = []
        for g in range(N_KV):
            m_old = carry[g]
            m_new = m_old
            for i in range(ATT_TILES):
                lhs = jnp.concatenate([pick[i * QB:(i + 1) * QB], kb_ref[g, pl.ds(r0 + i * QB, QB), :]], axis=1)
                s = jnp.dot(lhs, q_aug[g], preferred_element_type=F32)
                m_new = jnp.maximum(m_new, s.max(axis=0, keepdims=True))
            lhs = jnp.concatenate([pick, kb_ref[g, pl.ds(r0, arows), :]], axis=1)
            p = jnp.exp2(jnp.dot(lhs, q_aug[g], preferred_element_type=F32) - m_new)
            vt = jnp.concatenate([vt_ref[u * ATT_TILES + i, g] for i in range(ATT_TILES)], axis=1)
            accumulate(g, p, vt, m_old, m_new)
            out.append(m_new)
        return tuple(out)

    m_far = lax.fori_loop(0, (nt + 1) // ATT_TILES, far_step, (jnp.full((1, GROUP * QB), NEG_BIG, F32),) * N_KV)

    near = (jnp.maximum(nt - 2, 0), nt - 1)

    def near_pick(tile):
        r0 = pl.multiple_of(tile * QB, QB)
        key = keys_ref[pl.ds(r0, QB), :]
        return jnp.where(key >= thr, jnp.where(admissible(r0, QB), 0.0, MASK_NEG), MASK_NEG)

    pick = jnp.concatenate([jnp.where(nt >= 2, near_pick(near[0]), MASK_NEG), near_pick(near[1])], axis=0).astype(BF16)
    for g in range(N_KV):
        kk = jnp.concatenate([kb_ref[g, pl.ds(pl.multiple_of(tl * QB, QB), QB), :] for tl in near], axis=0)
        bias = jnp.concatenate([bias_ref[g, 0], bias_ref[g, 1]], axis=0)
        s = jnp.dot(jnp.concatenate([pick, kk], axis=1), q_aug[g], preferred_element_type=F32) + bias
        m_new = jnp.maximum(m_far[g], s.max(axis=0, keepdims=True))
        vt = jnp.concatenate([vt_ref[tl, g] for tl in near], axis=1)
        accumulate(g, jnp.exp2(s - m_new), vt, m_far[g], m_new)

    heads = []
    for g in range(N_KV):
        og = acc_ref[g, 0:HEAD_DIM, :] / acc_ref[g, HEAD_DIM:HEAD_DIM + 1, :]
        heads += [og[:, hh * QB:(hh + 1) * QB] for hh in range(GROUP)]
    o_ref[...] = jnp.concatenate(heads, axis=0).T


def _dsa_key_rows(n_qb, tile_base):
    return -(-(tile_base + n_qb) // SCORE_TILES) * SCORE_TILES * QB


def _dsa(qt, qit, wit, k, v, ki, bias, *, n_qb, q_base, tile_base, l_valid):
    b, lk, _ = k.shape
    assert lk == _dsa_key_rows(n_qb, tile_base)
    n_tiles_pad = lk // QB
    topk = min(TOPK_MAX, l_valid // 4)
    body = functools.partial(_dsa_body, q_base=q_base, tile_base=tile_base, l_valid=l_valid, topk=topk,
                             idx_bits=max(1, (lk - 1).bit_length()), n_tiles_pad=n_tiles_pad)
    tok = lambda bi, qi: (0, bi * n_qb + qi)
    per_batch = lambda bi, qi: (bi, 0, 0)
    return pl.pallas_call(
        body,
        grid=(b, n_qb),
        in_specs=[pl.BlockSpec((WIDTH_A, QB), tok),
                  pl.BlockSpec((IQ_W, QB), tok),
                  pl.BlockSpec((H_IDX, QB), tok),
                  pl.BlockSpec((None, lk, KV_W), per_batch),
                  pl.BlockSpec((None, lk, KV_W), per_batch),
                  pl.BlockSpec((None, lk, D_IDX), per_batch),
                  _resident((N_KV, 2, QB, GROUP * QB))],
        out_specs=pl.BlockSpec((QB, WIDTH_A), lambda bi, qi: (bi * n_qb + qi, 0)),
        out_shape=jax.ShapeDtypeStruct((b * n_qb * QB, WIDTH_A), F32),
        scratch_shapes=[pltpu.VMEM((lk, QB), I32),
                        pltpu.VMEM((lk, QB), I16),
                        pltpu.VMEM((lk, QB), I16),
                        pltpu.VMEM((N_KV, lk, LANES), BF16),
                        pltpu.VMEM((lk, D_IDX), BF16),
                        pltpu.VMEM((n_tiles_pad, N_KV, VT_ROWS, QB), BF16),
                        pltpu.VMEM((N_KV, VT_ROWS, GROUP * QB), F32)],
        compiler_params=_params("arbitrary", "arbitrary"),
        name="dsa_attention",
    )(qt, qit, wit, k, v, ki, bias)


def _split3(x):
    hi = x.astype(BF16)
    r1 = x - hi.astype(F32)
    mid = r1.astype(BF16)
    return hi, mid, (r1 - mid.astype(F32)).astype(BF16)


def _hgrn_conv_body(pbc_ref, lbl_ref, gn_ref, cw_ref, cb_ref, lng_ref, lnb_ref, s0_ref, cbuf_ref,
                    obc_ref, sout_ref, cout_ref,
                    st_ref, ubuf_ref, xbuf_ref, r_ref, a_ref, qs_ref, kk_ref, dd_ref,
                    kmask_ref, qmask_ref, stk_ref, ut_ref, oi_ref, *, layer, tm):
    t = pl.program_id(1)
    n_sub = tm // SUB
    hw = H_B * DK_B
    ri = lax.broadcasted_iota(I32, (hw, hw), 0)
    ci = lax.broadcasted_iota(I32, (hw, hw), 1)
    bd = jnp.where((ri // DK_B) == (ci // DK_B), 1.0, 0.0)

    @pl.when((pl.program_id(0) == 0) & (t == 0))
    def _():
        kmask_ref[...] = jnp.zeros(kmask_ref.shape, BF16)
        qmask_ref[...] = jnp.zeros(qmask_ref.shape, BF16)
        stk_ref[...] = jnp.zeros(stk_ref.shape, BF16)

    @pl.when(t == 0)
    def _():
        s_bd = jnp.concatenate([s0_ref[...]] * H_B, axis=1) * bd
        st_ref[...] = s_bd.T
        ubuf_ref[0:HALO - (CONV_W - 1), :] = jnp.zeros((HALO - (CONV_W - 1), C_CONV), F32)
        ubuf_ref[HALO - (CONV_W - 1):HALO, :] = cbuf_ref[...]

    @pl.when(t > 0)
    def _():
        ubuf_ref[0:HALO, :] = ubuf_ref[tm:tm + HALO, :]

    lg = lbl_ref[...]
    e = jnp.exp(lg - lg.max(axis=0, keepdims=True))
    p = e / e.sum(axis=0, keepdims=True)
    lb = jnp.zeros((1, hw), F32)
    for m in range(1, layer + 1):
        lb = lb + p[m:m + 1, :]

    qraw = pbc_ref[:, 0:hw]
    z = pbc_ref[:, hw:2 * hw]
    qs_ref[...] = qraw * _sigmoid(qraw)
    log_sig = -(jnp.maximum(-z, 0.0) + jnp.log1p(jnp.exp(-jnp.abs(z))))
    a1 = jnp.log(lb)
    b1 = jnp.log1p(-lb) + log_sig
    log_f = jnp.maximum(a1, b1) + jnp.log1p(jnp.exp(-jnp.abs(a1 - b1)))
    kk_ref[...] = (1.0 - lb) * _sigmoid(-z)

    tr = lax.broadcasted_iota(I32, (2 * tm, tm), 0)
    tc = lax.broadcasted_iota(I32, (2 * tm, tm), 1)
    tr_tok = jnp.where(tr < tm, tr, tr - tm)
    ones_mat = jnp.where((tr_tok // SUB) == (tc // SUB), jnp.where(tr < tm, jnp.where(tc <= tr_tok, 1.0, 0.0), 1.0), 0.0)
    ones_mat = ones_mat.astype(BF16)
    cum = sum(jnp.dot(ones_mat, part, preferred_element_type=F32) for part in _split3(log_f))
    a_loc, a_end = cum[0:tm], cum[tm:2 * tm]
    a_ref[...] = a_loc
    dd_ref[...] = jnp.exp(a_end)
    qd = (qs_ref[...] * jnp.exp(a_loc)).astype(BF16)
    kd = (kk_ref[...] * jnp.exp(a_end - a_loc)).astype(BF16)
    for s in range(n_sub):
        qmask_ref[s * SUB:(s + 1) * SUB, s * hw:(s + 1) * hw] = qd[s * SUB:(s + 1) * SUB]
        kmask_ref[s * SUB:(s + 1) * SUB, s * hw:(s + 1) * hw] = kd[s * SUB:(s + 1) * SUB]

    ii = lax.broadcasted_iota(I32, (SUB, hw), 0)

    def build_x(s, carry):
        r0 = pl.multiple_of(s * SUB, SUB)
        a_s = a_ref[pl.ds(r0, SUB), :]
        q_s = qs_ref[pl.ds(r0, SUB), :]
        for j in range(SUB):
            a_j = a_ref[pl.ds(r0 + j, 1), :]
            k_j = kk_ref[pl.ds(r0 + j, 1), :]
            dec = jnp.exp(jnp.where(ii >= j, a_s - a_j, -jnp.inf))
            x0 = pl.multiple_of(s * SUB * SUB + j * SUB, SUB)
            xbuf_ref[pl.ds(x0, SUB), :] = (q_s * k_j * dec).astype(BF16)
        return carry

    lax.fori_loop(0, n_sub, build_x, 0)
    r_ref[...] = jnp.dot(xbuf_ref[...], bd.astype(BF16), preferred_element_type=F32)

    def intra(s, carry):
        r0 = pl.multiple_of(s * SUB, SUB)
        o = jnp.zeros((SUB, hw), F32)
        for j in range(SUB):
            x0 = pl.multiple_of(s * SUB * SUB + j * SUB, SUB)
            o = o + r_ref[pl.ds(x0, SUB), :] * pbc_ref[pl.ds(r0 + j, 1), 2 * hw:3 * hw]
        oi_ref[pl.ds(r0, SUB), :] = o
        return carry

    lax.fori_loop(0, n_sub, intra, 0)

    val = pbc_ref[:, 2 * hw:3 * hw].astype(BF16)
    ut_ref[...] = lax.dot_general(val, kmask_ref[...], TN_DIMS, preferred_element_type=F32)
    slabs = []
    for h in range(H_B):
        lt = (h * DK_B) // LANES * LANES
        lane = lax.broadcasted_iota(I32, (DV_B, LANES), 1) + lt
        slabs.append((lt, jnp.where(lane // DK_B == h, 1.0, 0.0), st_ref[h * DV_B:(h + 1) * DV_B, lt:lt + LANES]))
    for s in range(n_sub):
        nxt = []
        for h, (lt, own, st) in enumerate(slabs):
            c0 = s * hw + lt
            stk_ref[h * DV_B:(h + 1) * DV_B, c0:c0 + LANES] = st.astype(BF16)
            st = st * dd_ref[s * SUB:s * SUB + 1, lt:lt + LANES] + ut_ref[h * DV_B:(h + 1) * DV_B, c0:c0 + LANES] * own
            nxt.append((lt, own, st))
        slabs = nxt
    for h, (lt, own, st) in enumerate(slabs):
        st_ref[h * DV_B:(h + 1) * DV_B, lt:lt + LANES] = st
    o = oi_ref[...] + lax.dot_general(qmask_ref[...], stk_ref[...], NT_DIMS, preferred_element_type=F32)

    ms = jnp.dot((o * o).astype(BF16), bd.astype(BF16), preferred_element_type=F32) * (1.0 / DV_B)
    gate = pbc_ref[:, 3 * hw:4 * hw]
    obc_ref[:, 0:WIDTH_B] = o * lax.rsqrt(ms + EPS) * gn_ref[...] * (gate * _sigmoid(gate))

    c0 = 4 * hw
    u = pbc_ref[:, c0:c0 + C_CONV] * _sigmoid(pbc_ref[:, c0 + C_CONV:c0 + 2 * C_CONV])
    ubuf_ref[HALO:HALO + tm, :] = u
    y = jnp.zeros((tm, C_CONV), F32)
    first = HALO - (CONV_W - 1)
    for r in range(SUBLANES):
        taps = range(r, CONV_W, SUBLANES)
        win = ubuf_ref[pl.ds(first + r, tm + SUBLANES * (len(taps) - 1)), :]
        for n, w in enumerate(taps):
            y = y + win[SUBLANES * n:SUBLANES * n + tm] * cw_ref[w:w + 1, :]
    y = y + cb_ref[...]
    yc = y - jnp.mean(y, axis=-1, keepdims=True)
    ln = yc * lax.rsqrt(jnp.mean(yc * yc, axis=-1, keepdims=True) + EPS) * lng_ref[...] + lnb_ref[...]
    obc_ref[:, WIDTH_B:WIDTH_BC] = ln * _sigmoid(ln)

    @pl.when(t == pl.num_programs(1) - 1)
    def _():
        s_bd = st_ref[...].T
        s_new = s_bd[:, 0:DV_B]
        for h in range(1, H_B):
            s_new = s_new + s_bd[:, h * DV_B:(h + 1) * DV_B]
        sout_ref[...] = s_new
        cout_ref[...] = ubuf_ref[tm + HALO - (CONV_W - 1):tm + HALO, :]


def _hgrn_conv(pbc, lb_logits, gn, cw, cb, lng, lnb, s0, cbuf, *, layer, seq, tm):
    b = s0.shape[0]
    n_t = seq // tm
    n_sub = tm // SUB
    hw = H_B * DK_B
    body = functools.partial(_hgrn_conv_body, layer=layer, tm=tm)
    per_batch = lambda bi, ti: (bi, 0, 0)
    tok = lambda bi, ti: (bi * n_t + ti, 0)
    return pl.pallas_call(
        body,
        grid=(b, n_t),
        in_specs=[pl.BlockSpec((tm, BC_W), tok),
                  _resident((DEPTH, hw)),
                  _resident((1, hw)),
                  _resident((CONV_W, C_CONV)),
                  _resident((1, C_CONV)),
                  _resident((1, C_CONV)),
                  _resident((1, C_CONV)),
                  pl.BlockSpec((None, hw, DV_B), per_batch),
                  pl.BlockSpec((None, CONV_W - 1, C_CONV), per_batch)],
        out_specs=[pl.BlockSpec((tm, WIDTH_BC), tok),
                   pl.BlockSpec((None, hw, DV_B), per_batch),
                   pl.BlockSpec((None, CONV_W - 1, C_CONV), per_batch)],
        out_shape=[jax.ShapeDtypeStruct((b * seq, WIDTH_BC), F32),
                   jax.ShapeDtypeStruct((b, hw, DV_B), F32),
                   jax.ShapeDtypeStruct((b, CONV_W - 1, C_CONV), F32)],
        scratch_shapes=[pltpu.VMEM((hw, hw), F32),
                        pltpu.VMEM((tm + HALO, C_CONV), F32),
                        pltpu.VMEM((n_sub * SUB * SUB, hw), BF16),
                        pltpu.VMEM((n_sub * SUB * SUB, hw), F32),
                        pltpu.VMEM((tm, hw), F32),
                        pltpu.VMEM((tm, hw), F32),
                        pltpu.VMEM((tm, hw), F32),
                        pltpu.VMEM((tm, hw), F32),
                        pltpu.VMEM((tm, n_sub * hw), BF16),
                        pltpu.VMEM((tm, n_sub * hw), BF16),
                        pltpu.VMEM((hw, n_sub * hw), BF16),
                        pltpu.VMEM((hw, n_sub * hw), F32),
                        pltpu.VMEM((tm, hw), F32)],
        compiler_params=_params("arbitrary", "arbitrary"),
        name="hgrn_conv",
    )(pbc, lb_logits, gn, cw, cb, lng, lnb, s0, cbuf)


def _outproj_body(x_ref, oa_ref, obc_ref, wa_ref, wbc_ref, g_ref, o_ref):
    m = jnp.dot(oa_ref[...].astype(BF16), wa_ref[...], preferred_element_type=F32)
    m = m + jnp.dot(obc_ref[...].astype(BF16), wbc_ref[...], preferred_element_type=F32)
    o_ref[...] = x_ref[...] + _rms(m, g_ref[...])


def _out_proj(x, oa, obc, wa, wbc, g, tm):
    t = x.shape[0]
    row = lambda i: (i, 0)
    return pl.pallas_call(
        _outproj_body,
        grid=(t // tm,),
        in_specs=[pl.BlockSpec((tm, D_MODEL), row),
                  pl.BlockSpec((tm, WIDTH_A), row),
                  pl.BlockSpec((tm, WIDTH_BC), row),
                  _resident((WIDTH_A, D_MODEL)),
                  _resident((WIDTH_BC, D_MODEL)),
                  _resident((1, D_MODEL))],
        out_specs=pl.BlockSpec((tm, D_MODEL), row),
        out_shape=jax.ShapeDtypeStruct((t, D_MODEL), F32),
        compiler_params=_params("arbitrary"),
        name="out_proj",
    )(x, oa, obc, wa, wbc, g)


def _split_w_in(w_in):
    sizes = (WIDTH_A, KV_W, KV_W, IQ_W, D_IDX, H_IDX)
    offs = [0]
    for s in sizes:
        offs.append(offs[-1] + s)
    wb = w_in.astype(BF16)
    q, k, v, qi, ki, wi = (wb[:, offs[n]:offs[n + 1]] for n in range(len(sizes)))
    return {"q_t": q.T, "qi_t": qi.T, "wi_t": wi.T, "k": k, "v": v, "ki": ki, "bc": wb[:, offs[-1]:]}


def _layer(x, past, lw, bias, *, layer, batch, seq, tm_tok, tm_seq):
    x = _ffn_half(x, lw["norm_ffn1"], lw["ffn1_gu"], lw["ffn1_down"], tm_tok)
    qt, qit, wit, k_new, v_new, ki_new, pbc = _in_proj(x, lw["norm_mix"][0:1], lw["w_in"], tm_tok)
    hw = H_B * DK_B
    if past is None:
        assert seq == _dsa_key_rows(seq // QB, 0)
        o_a = _dsa(qt, qit, wit, k_new.reshape(batch, seq, KV_W), v_new.reshape(batch, seq, KV_W),
                   ki_new.reshape(batch, seq, D_IDX), bias, n_qb=seq // QB, q_base=0, tile_base=0, l_valid=seq)
        s0 = jnp.zeros((batch, hw, DV_B), F32)
        cbuf = jnp.zeros((batch, CONV_W - 1, C_CONV), F32)
    else:
        k_c, v_c, ki_c, s0, cbuf = past
        past_len = k_c.shape[1]
        assert past_len % QB == 0 and seq <= QB
        l_valid = past_len + seq
        rows = _dsa_key_rows(1, past_len // QB)
        pad_k = lambda c, n, w: jnp.pad(jnp.concatenate([c.reshape(batch, past_len, w), n.reshape(batch, seq, w)], axis=1),
                                        ((0, 0), (0, rows - l_valid), (0, 0)))
        pad_q = lambda a: jnp.pad(a.reshape(a.shape[0], batch, seq), ((0, 0), (0, 0), (0, QB - seq))).reshape(a.shape[0], batch * QB)
        o_a = _dsa(pad_q(qt), pad_q(qit), pad_q(wit), pad_k(k_c, k_new, KV_W), pad_k(v_c, v_new, KV_W),
                   pad_k(ki_c, ki_new, D_IDX), bias, n_qb=1, q_base=past_len, tile_base=past_len // QB, l_valid=l_valid)
        o_a = o_a.reshape(batch, QB, WIDTH_A)[:, :seq].reshape(batch * seq, WIDTH_A)
        s0 = s0.reshape(batch, hw, DV_B)
    obc, s_new, c_new = _hgrn_conv(pbc, lw["lb_logits"], lw["gnorm"], lw["conv_w"], lw["conv_b"], lw["ln_g"], lw["ln_b"],
                                   s0, cbuf, layer=layer, seq=seq, tm=tm_seq)
    x = _out_proj(x, o_a, obc, lw["w_out_a"], lw["w_out_bc"], lw["norm_mix"][1:2], tm_tok)
    x = _ffn_half(x, lw["norm_ffn2"], lw["ffn2_gu"], lw["ffn2_down"], tm_tok)
    state = (k_new.reshape(batch, seq, N_KV, HEAD_DIM), v_new.reshape(batch, seq, N_KV, HEAD_DIM),
             ki_new.reshape(batch, seq, D_IDX), s_new.reshape(batch, H_B, DK_B, DV_B), c_new)
    return x, state


def kernel(x_prompt, x_sample, cache_attn_k, cache_attn_v, cache_idx_k, state_hgrn, state_conv, norm_ffn1, ffn1_w_gate_up, ffn1_w_down, norm_mix, w_in, w_out, rel_bias, hgrn_lb_logits, hgrn_gnorm, conv_w, conv_b, conv_ln_g, conv_ln_b, norm_ffn2, ffn2_w_gate_up, ffn2_w_down):
    bp, sp, _ = x_prompt.shape
    bs, ss, _ = x_sample.shape
    bias = _bias_tiles(rel_bias)
    xp = x_prompt.reshape(bp * sp, D_MODEL)
    xs = x_sample.reshape(bs * ss, D_MODEL)
    st_p, st_s = [], []
    for l in range(DEPTH):
        lw = {
            "norm_ffn1": norm_ffn1[l], "ffn1_gu": ffn1_w_gate_up[l].astype(BF16), "ffn1_down": ffn1_w_down[l].astype(BF16),
            "norm_mix": norm_mix[l], "w_in": _split_w_in(w_in[l]),
            "w_out_a": w_out[l, :WIDTH_A].astype(BF16), "w_out_bc": w_out[l, WIDTH_A:].astype(BF16),
            "lb_logits": hgrn_lb_logits, "gnorm": jnp.tile(hgrn_gnorm[l], H_B)[None, :],
            "conv_w": conv_w[l], "conv_b": conv_b[l][None, :], "ln_g": conv_ln_g[l][None, :], "ln_b": conv_ln_b[l][None, :],
            "norm_ffn2": norm_ffn2[l], "ffn2_gu": ffn2_w_gate_up[l].astype(BF16), "ffn2_down": ffn2_w_down[l].astype(BF16),
        }
        xp, sp_state = _layer(xp, None, lw, bias, layer=l, batch=bp, seq=sp, tm_tok=min(512, bp * sp), tm_seq=min(256, sp))
        past = (cache_attn_k[l], cache_attn_v[l], cache_idx_k[l], state_hgrn[l], state_conv[l])
        xs, ss_state = _layer(xs, past, lw, bias, layer=l, batch=bs, seq=ss, tm_tok=min(512, bs * ss), tm_seq=min(256, ss))
        st_p.append(sp_state)
        st_s.append(ss_state)
    stack = lambda sts, j: jnp.stack([s[j] for s in sts], axis=0)
    return (xp.reshape(bp, sp, D_MODEL), xs.reshape(bs, ss, D_MODEL),
            *(stack(st_p, j) for j in range(5)), *(stack(st_s, j) for j in range(5)))
```

```python
import functools
import math

import jax
import jax.numpy as jnp
from jax import lax
from jax.experimental import pallas as pl
from jax.experimental.pallas import tpu as pltpu

D_MODEL = 1024
DEPTH = 2
CHUNK = 64
CHUNK_SHIFT = 6
H_A = 8
HEAD_DIM = 64
N_KV = 2
GROUP = H_A // N_KV
H_IDX = 8
D_IDX = 64
TOPK_MAX = 256
NUM_BUCKETS = 32
MAX_DISTANCE = 128
H_B = 4
DK_B = 64
DV_B = 64
C_CONV = 256
CONV_W = 31
D_FF = 2816
EPS = 1e-6

WIDTH_A = H_A * HEAD_DIM
WIDTH_B = H_B * DV_B
WIDTH_BC = WIDTH_B + C_CONV
KV_W = N_KV * HEAD_DIM
IQ_W = H_IDX * D_IDX
BC_W = 4 * WIDTH_B + 2 * C_CONV

F32 = jnp.float32
BF16 = jnp.bfloat16
I32 = jnp.int32
I16 = jnp.int16

LANES = 128
SUBLANES = 8
PACK16 = 2 * SUBLANES
QB = LANES
SCORE_TILES = 2
ATT_TILES = 4
VT_ROWS = HEAD_DIM + PACK16
SUB = 16
HEADS_PER_PAIR = LANES // DK_B
N_PAIR = H_B // HEADS_PER_PAIR
HALO = 32
FFN_CHUNK = 256
VMEM_LIMIT = 56 * 1024 * 1024
INT_MIN = -2 ** 31
NEG_INF_KEY = INT_MIN + 0x7FFFFF
I16_MIN = -2 ** 15
I16_MAX = 2 ** 15 - 1
NEG_BIG = -1e30
MASK_NEG = -2.0 ** 100
LOG2E = 1.4426950408889634

NT_DIMS = (((1,), (1,)), ((), ()))
TN_DIMS = (((0,), (0,)), ((), ()))


def _bucket_thresholds():
    half = NUM_BUCKETS // 2
    max_exact = half // 2

    def big(n):
        return min(max_exact + int(math.log(n / max_exact) / math.log(MAX_DISTANCE / max_exact) * (half - max_exact)),
                   half - 1)

    return tuple(min(n for n in range(max_exact, 4 * MAX_DISTANCE) if big(n) >= b) for b in range(max_exact + 1, half))


BUCKET_THRESHOLDS = _bucket_thresholds()


def _params(*sem):
    return pltpu.CompilerParams(dimension_semantics=sem, vmem_limit_bytes=VMEM_LIMIT)


def _resident(shape):
    return pl.BlockSpec(shape, lambda *_: (0,) * len(shape), pipeline_mode=pl.Buffered(1))


def _rms(x, g):
    return x * lax.rsqrt(jnp.mean(x * x, axis=-1, keepdims=True) + EPS) * g


def _sigmoid(x):
    return 0.5 * jnp.tanh(0.5 * x) + 0.5


def _sum_leading(x):
    parts = [x[i] for i in range(x.shape[0])]
    while len(parts) > 1:
        parts = [a + b for a, b in zip(parts[0::2], parts[1::2])] + ([parts[-1]] if len(parts) % 2 else [])
    return parts[0]


def _ffn_body(x_ref, n_ref, wgu_ref, wd_ref, o_ref, acc_ref):
    x = x_ref[...]
    h = _rms(x, n_ref[0:1, :]).astype(BF16)
    for c in range(D_FF // FFN_CHUNK):
        lo, hi = c * FFN_CHUNK, (c + 1) * FFN_CHUNK
        g = jnp.dot(h, wgu_ref[:, lo:hi], preferred_element_type=F32)
        u = jnp.dot(h, wgu_ref[:, D_FF + lo:D_FF + hi], preferred_element_type=F32)
        a = (g * _sigmoid(g) * u).astype(BF16)
        part = jnp.dot(a, wd_ref[lo:hi, :], preferred_element_type=F32)
        if c == 0:
            acc_ref[...] = part
        else:
            acc_ref[...] += part
    o_ref[...] = x + 0.5 * _rms(acc_ref[...], n_ref[1:2, :])


def _ffn_half(x, norms, w_gu, w_down, tm):
    t = x.shape[0]
    return pl.pallas_call(
        _ffn_body,
        grid=(t // tm,),
        in_specs=[pl.BlockSpec((tm, D_MODEL), lambda i: (i, 0)),
                  _resident((2, D_MODEL)),
                  _resident((D_MODEL, 2 * D_FF)),
                  _resident((D_FF, D_MODEL))],
        out_specs=pl.BlockSpec((tm, D_MODEL), lambda i: (i, 0)),
        out_shape=jax.ShapeDtypeStruct((t, D_MODEL), F32),
        scratch_shapes=[pltpu.VMEM((tm, D_MODEL), F32)],
        compiler_params=_params("arbitrary"),
        name="ffn_half",
    )(x, norms, w_gu, w_down)


def _inproj_body(x_ref, g_ref, wq_ref, wqi_ref, wwi_ref, wk_ref, wv_ref, wki_ref, wbc_ref,
                 qt_ref, qit_ref, wit_ref, k_ref, v_ref, ki_ref, pbc_ref):
    h = _rms(x_ref[...], g_ref[...]).astype(BF16)
    qt = lax.dot_general(wq_ref[...], h, NT_DIMS, preferred_element_type=F32)
    qt_ref[...] = (qt * (HEAD_DIM ** -0.5 * LOG2E)).astype(BF16)
    qit_ref[...] = lax.dot_general(wqi_ref[...], h, NT_DIMS, preferred_element_type=F32).astype(BF16)
    wit_ref[...] = lax.dot_general(wwi_ref[...], h, NT_DIMS, preferred_element_type=F32) * (H_IDX ** -0.5)
    k_ref[...] = jnp.dot(h, wk_ref[...], preferred_element_type=F32)
    v_ref[...] = jnp.dot(h, wv_ref[...], preferred_element_type=F32)
    ki_ref[...] = jnp.dot(h, wki_ref[...], preferred_element_type=F32)
    pbc_ref[...] = jnp.dot(h, wbc_ref[...], preferred_element_type=F32)


def _in_proj(x, g, w, tm):
    t = x.shape[0]
    row = lambda i: (i, 0)
    col = lambda i: (0, i)
    return pl.pallas_call(
        _inproj_body,
        grid=(t // tm,),
        in_specs=[pl.BlockSpec((tm, D_MODEL), row),
                  _resident((1, D_MODEL)),
                  _resident((WIDTH_A, D_MODEL)),
                  _resident((IQ_W, D_MODEL)),
                  _resident((H_IDX, D_MODEL)),
                  _resident((D_MODEL, KV_W)),
                  _resident((D_MODEL, KV_W)),
                  _resident((D_MODEL, D_IDX)),
                  _resident((D_MODEL, BC_W))],
        out_specs=[pl.BlockSpec((WIDTH_A, tm), col),
                   pl.BlockSpec((IQ_W, tm), col),
                   pl.BlockSpec((H_IDX, tm), col),
                   pl.BlockSpec((tm, KV_W), row),
                   pl.BlockSpec((tm, KV_W), row),
                   pl.BlockSpec((tm, D_IDX), row),
                   pl.BlockSpec((tm, BC_W), row)],
        out_shape=[jax.ShapeDtypeStruct((WIDTH_A, t), BF16),
                   jax.ShapeDtypeStruct((IQ_W, t), BF16),
                   jax.ShapeDtypeStruct((H_IDX, t), F32),
                   jax.ShapeDtypeStruct((t, KV_W), F32),
                   jax.ShapeDtypeStruct((t, KV_W), F32),
                   jax.ShapeDtypeStruct((t, D_IDX), F32),
                   jax.ShapeDtypeStruct((t, BC_W), F32)],
        compiler_params=_params("arbitrary"),
        name="in_proj",
    )(x, g, w["q_t"], w["qi_t"], w["wi_t"], w["k"], w["v"], w["ki"], w["bc"])


def _bias_body(rb_ref, o_ref):
    j = lax.broadcasted_iota(I32, (QB, QB), 0)
    i = lax.broadcasted_iota(I32, (QB, QB), 1)
    half = NUM_BUCKETS // 2
    max_exact = half // 2
    for var, off in ((0, -QB), (1, 0)):
        rel = j - i + off
        n = jnp.abs(rel)
        big = jnp.full((QB, QB), max_exact, I32)
        for thr in BUCKET_THRESHOLDS:
            big = big + jnp.where(n >= thr, 1, 0)
        bucket = jnp.where(n < max_exact, n, big) + jnp.where(rel > 0, half, 0)
        for head in range(H_A):
            g, hh = divmod(head, GROUP)
            far = rb_ref[half - 1, head]
            tile = jnp.zeros((QB, QB), F32)
            for b in range(NUM_BUCKETS):
                tile = jnp.where(bucket == b, (rb_ref[b, head] - far) * LOG2E, tile)
            o_ref[g, var, :, hh * QB:(hh + 1) * QB] = tile


def _bias_tiles(rel_bias):
    return pl.pallas_call(
        _bias_body,
        in_specs=[pl.BlockSpec(memory_space=pltpu.SMEM)],
        out_shape=jax.ShapeDtypeStruct((N_KV, 2, QB, GROUP * QB), F32),
        name="rel_bias_tiles",
    )(rel_bias)


def _dsa_body(qt_ref, qit_ref, wit_ref, k_ref, v_ref, ki_ref, bias_ref, o_ref,
              keys_ref, khi_ref, klo_ref, kb_ref, kib_ref, vt_ref, acc_ref, s_ref, *,
              q_base, tile_base, l_valid, topk, idx_bits, n_tiles_pad):
    qb = pl.program_id(1)
    nt = tile_base + qb + 1
    srows = SCORE_TILES * QB
    arows = ATT_TILES * QB

    @pl.when(qb == 0)
    def _():
        ones_rows = jnp.where(lax.broadcasted_iota(I32, (PACK16, QB), 0) == 0, 1.0, 0.0)
        for t in range(n_tiles_pad):
            rows = slice(t * QB, (t + 1) * QB)
            kt = k_ref[rows, :]
            vt = v_ref[rows, :].T
            for g in range(N_KV):
                kg = jnp.concatenate([kt[:, g * HEAD_DIM:(g + 1) * HEAD_DIM], jnp.zeros((QB, LANES - HEAD_DIM), F32)], axis=1)
                kb_ref[g, rows, :] = kg.astype(BF16)
                vt_ref[t, g] = jnp.concatenate([vt[g * HEAD_DIM:(g + 1) * HEAD_DIM, :], ones_rows], axis=0).astype(BF16)
            kib_ref[rows, :] = ki_ref[rows, :].astype(BF16)

    def admissible(r0, rows):
        k_pos = r0 + lax.broadcasted_iota(I32, (rows, QB), 0)
        q_chunk = (q_base + qb * QB + lax.broadcasted_iota(I32, (rows, QB), 1)) >> CHUNK_SHIFT
        return jnp.where(k_pos < l_valid, k_pos >> CHUNK_SHIFT, 2 ** 30) <= q_chunk

    n2 = (nt + SCORE_TILES - 1) // SCORE_TILES

    qi_pairs = [jnp.concatenate([qit_ref[h * D_IDX:(h + 1) * D_IDX, :] for h in (hp, hp + 1)], axis=1)
                for hp in range(0, H_IDX, 2)]

    def score_step(u, carry):
        r0 = pl.multiple_of(u * srows, srows)
        ki = kib_ref[pl.ds(r0, srows), :]
        sc = None
        for n, qi in enumerate(qi_pairs):
            lg = jnp.dot(ki, qi, preferred_element_type=F32)
            for hh in range(2):
                w = jnp.maximum(lg[:, hh * QB:(hh + 1) * QB], 0.0) * wit_ref[2 * n + hh:2 * n + hh + 1, :]
                sc = w if sc is None else sc + w
        sc = jnp.where(admissible(r0, srows), sc + 0.0, -jnp.inf)
        bits = pltpu.bitcast(sc, I32)
        key = bits ^ ((bits >> 31) & 0x7FFFFFFF)
        keys_ref[pl.ds(r0, srows), :] = key
        khi_ref[pl.ds(r0, srows), :] = (key >> 16).astype(I16)
        klo_ref[pl.ds(r0, srows), :] = ((key & 0xFFFF) + I16_MIN).astype(I16)
        return carry

    lax.fori_loop(0, n2, score_step, 0)

    def count32(hit_fn):
        def body(t, acc):
            r0 = pl.multiple_of(t * srows, srows)
            hit = hit_fn(keys_ref[pl.ds(r0, srows), :], r0)
            return acc + hit.reshape(srows // SUBLANES, SUBLANES, QB).sum(axis=0)

        acc = lax.fori_loop(0, n2, body, jnp.zeros((SUBLANES, QB), I32))
        return acc.sum(axis=0, keepdims=True)

    def count16(ref, hit_fn):
        def body(t, acc):
            r0 = pl.multiple_of(t * srows, srows)
            hit = hit_fn(ref[pl.ds(r0, srows), :].reshape(srows // PACK16, PACK16, QB))
            return acc + _sum_leading(hit)

        acc = lax.fori_loop(0, n2, body, jnp.zeros((PACK16, QB), I16))
        return acc.astype(I32).sum(axis=0, keepdims=True)

    def bcast16(x):
        return jnp.broadcast_to(x, (PACK16, QB)).astype(I16)[None]

    one16 = jnp.ones((1, PACK16, QB), I16)
    zero16 = jnp.zeros((1, PACK16, QB), I16)

    def count16_ge(ref, cand):
        c16 = bcast16(cand)
        return count16(ref, lambda blk: jnp.where(blk >= c16, one16, zero16))

    def bisect16(count_ge, total):
        c0, need0 = count_ge(jnp.zeros((1, QB), I32))
        val = jnp.where(c0 >= need0, 0, I16_MIN)
        cnt = jnp.where(c0 >= need0, c0, total)

        def step(bi, carry):
            val, cnt = carry
            cand = val + lax.shift_left(jnp.int32(1), 14 - bi)
            c, need = count_ge(cand)
            return jnp.where(c >= need, cand, val), jnp.where(c >= need, c, cnt)

        return lax.fori_loop(0, 15, step, (val, cnt))

    t_hi, _ = bisect16(lambda cand: (count16_ge(khi_ref, cand), topk), n2 * srows)
    above = jnp.where(t_hi == I16_MAX, 0, count16_ge(khi_ref, jnp.minimum(t_hi + 1, I16_MAX)))
    hi16 = bcast16(t_hi)
    low_min = jnp.full((1, PACK16, QB), I16_MIN, I16)

    def mask_low(t, acc):
        r0 = pl.multiple_of(t * srows, srows)
        same = khi_ref[pl.ds(r0, srows), :].reshape(srows // PACK16, PACK16, QB) == hi16
        lo = klo_ref[pl.ds(r0, srows), :].reshape(srows // PACK16, PACK16, QB)
        klo_ref[pl.ds(r0, srows), :] = jnp.where(same, lo, low_min).reshape(srows, QB)
        return acc + _sum_leading(jnp.where(same, one16, zero16))

    same_hi = lax.fori_loop(0, n2, mask_low, jnp.zeros((PACK16, QB), I16)).astype(I32).sum(axis=0, keepdims=True)
    need_lo = topk - above
    t_lo, cnt_lo = bisect16(lambda cand: (count16_ge(klo_ref, cand), need_lo), same_hi)
    thr = lax.shift_left(t_hi, 16) | (t_lo - I16_MIN)
    cnt = above + cnt_lo

    excess = jnp.where(thr > NEG_INF_KEY, cnt - topk, 0)

    @pl.when(jnp.max(excess) > 0)
    def _():
        need = topk - count32(lambda blk, r0: jnp.where(blk > thr, 1, 0))
        rowi = lax.broadcasted_iota(I32, (srows, QB), 0)

        def idx_step(bi, last):
            cand = last + lax.shift_left(jnp.int32(1), idx_bits - 1 - bi)
            c = count32(lambda blk, r0: jnp.where(blk == thr, jnp.where(r0 + rowi < cand, 1, 0), 0))
            return jnp.where(c < need, cand, last)

        last = lax.fori_loop(0, idx_bits, idx_step, jnp.zeros((1, QB), I32))

        def demote(t, carry):
            r0 = pl.multiple_of(t * srows, srows)
            blk = keys_ref[pl.ds(r0, srows), :]
            keys_ref[pl.ds(r0, srows), :] = jnp.where(blk == thr, jnp.where(r0 + rowi > last, thr - 1, blk), blk)
            return carry

        lax.fori_loop(0, n2, demote, 0)

    eye = jnp.where(lax.broadcasted_iota(I32, (QB, QB), 0) == lax.broadcasted_iota(I32, (QB, QB), 1), 1.0, 0.0)
    eye = jnp.concatenate([eye.astype(BF16)] * GROUP, axis=1)
    q_aug = []
    for g in range(N_KV):
        q_cat = jnp.concatenate([qt_ref[(g * GROUP + hh) * HEAD_DIM:(g * GROUP + hh + 1) * HEAD_DIM, :]
                                 for hh in range(GROUP)], axis=1)
        q_aug.append(jnp.concatenate([eye, q_cat, jnp.zeros((LANES - HEAD_DIM, GROUP * QB), BF16)], axis=0))
    acc_ref[...] = jnp.zeros(acc_ref.shape, F32)

    def softmax_update(g, load_s, vt, m_old):
        m_new = jnp.maximum(m_old, load_s().max(axis=0, keepdims=True))
        p = jnp.exp2(load_s() - m_new).astype(BF16)
        acc_ref[g] = acc_ref[g] * jnp.exp2(m_old - m_new) + jnp.dot(vt, p, preferred_element_type=F32)
        return m_new

    far_rows = (nt - 2) * QB
    row_a = lax.broadcasted_iota(I32, (arows, QB), 0)

    def far_step(u, carry):
        r0 = pl.multiple_of(u * arows, arows)
        key = keys_ref[pl.ds(r0, arows), :]
        pick = jnp.where(key >= thr, jnp.where(r0 + row_a < far_rows, 0.0, MASK_NEG), MASK_NEG).astype(BF16)
        for g in range(N_KV):
            lhs = jnp.concatenate([pick, kb_ref[g, pl.ds(r0, arows), :]], axis=1)
            s_ref[g] = jnp.dot(lhs, q_aug[g], preferred_element_type=F32)
        out = []
        for g in range(N_KV):
            vt = jnp.concatenate([vt_ref[u * ATT_TILES + i, g] for i in range(ATT_TILES)], axis=1)
            out.append(softmax_update(g, lambda g=g: s_ref[g], vt, carry[g]))
        return tuple(out)

    m_far = lax.fori_loop(0, (nt + 1) // ATT_TILES, far_step, (jnp.full((1, GROUP * QB), NEG_BIG, F32),) * N_KV)

    near = (jnp.maximum(nt - 2, 0), nt - 1)

    def near_pick(tile):
        r0 = pl.multiple_of(tile * QB, QB)
        key = keys_ref[pl.ds(r0, QB), :]
        return jnp.where(key >= thr, jnp.where(admissible(r0, QB), 0.0, MASK_NEG), MASK_NEG)

    pick = jnp.concatenate([jnp.where(nt >= 2, near_pick(near[0]), MASK_NEG), near_pick(near[1])], axis=0).astype(BF16)
    for g in range(N_KV):
        kk = jnp.concatenate([kb_ref[g, pl.ds(pl.multiple_of(tl * QB, QB), QB), :] for tl in near], axis=0)
        bias = jnp.concatenate([bias_ref[g, 0], bias_ref[g, 1]], axis=0)
        s = jnp.dot(jnp.concatenate([pick, kk], axis=1), q_aug[g], preferred_element_type=F32) + bias
        vt = jnp.concatenate([vt_ref[tl, g] for tl in near], axis=1)
        softmax_update(g, lambda s=s: s, vt, m_far[g])

    heads = []
    for g in range(N_KV):
        og = acc_ref[g, 0:HEAD_DIM, :] / acc_ref[g, HEAD_DIM:HEAD_DIM + 1, :]
        heads += [og[:, hh * QB:(hh + 1) * QB] for hh in range(GROUP)]
    o_ref[...] = jnp.concatenate(heads, axis=0).T


def _dsa_key_rows(n_qb, tile_base):
    return -(-(tile_base + n_qb) // SCORE_TILES) * SCORE_TILES * QB


def _dsa(qt, qit, wit, k, v, ki, bias, *, n_qb, q_base, tile_base, l_valid):
    b, lk, _ = k.shape
    assert lk == _dsa_key_rows(n_qb, tile_base)
    n_tiles_pad = lk // QB
    topk = min(TOPK_MAX, l_valid // 4)
    body = functools.partial(_dsa_body, q_base=q_base, tile_base=tile_base, l_valid=l_valid, topk=topk,
                             idx_bits=max(1, (lk - 1).bit_length()), n_tiles_pad=n_tiles_pad)
    tok = lambda bi, qi: (0, bi * n_qb + qi)
    per_batch = lambda bi, qi: (bi, 0, 0)
    return pl.pallas_call(
        body,
        grid=(b, n_qb),
        in_specs=[pl.BlockSpec((WIDTH_A, QB), tok),
                  pl.BlockSpec((IQ_W, QB), tok),
                  pl.BlockSpec((H_IDX, QB), tok),
                  pl.BlockSpec((None, lk, KV_W), per_batch),
                  pl.BlockSpec((None, lk, KV_W), per_batch),
                  pl.BlockSpec((None, lk, D_IDX), per_batch),
                  _resident((N_KV, 2, QB, GROUP * QB))],
        out_specs=pl.BlockSpec((QB, WIDTH_A), lambda bi, qi: (bi * n_qb + qi, 0)),
        out_shape=jax.ShapeDtypeStruct((b * n_qb * QB, WIDTH_A), F32),
        scratch_shapes=[pltpu.VMEM((lk, QB), I32),
                        pltpu.VMEM((lk, QB), I16),
                        pltpu.VMEM((lk, QB), I16),
                        pltpu.VMEM((N_KV, lk, LANES), BF16),
                        pltpu.VMEM((lk, D_IDX), BF16),
                        pltpu.VMEM((n_tiles_pad, N_KV, VT_ROWS, QB), BF16),
                        pltpu.VMEM((N_KV, VT_ROWS, GROUP * QB), F32),
                        pltpu.VMEM((N_KV, ATT_TILES * QB, GROUP * QB), F32)],
        compiler_params=_params("arbitrary", "arbitrary"),
        name="dsa_attention",
    )(qt, qit, wit, k, v, ki, bias)


def _split3(x):
    hi = x.astype(BF16)
    r1 = x - hi.astype(F32)
    mid = r1.astype(BF16)
    return hi, mid, (r1 - mid.astype(F32)).astype(BF16)


def _hgrn_conv_body(pbc_ref, lbl_ref, gn_ref, cw_ref, cb_ref, lng_ref, lnb_ref, s0_ref, cbuf_ref,
                    obc_ref, sout_ref, cout_ref,
                    st_ref, ubuf_ref, xbuf_ref, r_ref, a_ref, qs_ref, kk_ref, dd_ref,
                    kmask_ref, qmask_ref, stk_ref, ut_ref, oi_ref, *, layer, tm):
    t = pl.program_id(1)
    n_sub = tm // SUB
    hw = H_B * DK_B
    ri = lax.broadcasted_iota(I32, (LANES, LANES), 0)
    ci = lax.broadcasted_iota(I32, (LANES, LANES), 1)
    bd = jnp.where((ri // DK_B) == (ci // DK_B), 1.0, 0.0)
    bd16 = bd.astype(BF16)

    def pair(pp):
        return slice(pp * LANES, (pp + 1) * LANES)

    @pl.when((pl.program_id(0) == 0) & (t == 0))
    def _():
        kmask_ref[...] = jnp.zeros(kmask_ref.shape, BF16)
        qmask_ref[...] = jnp.zeros(qmask_ref.shape, BF16)

    @pl.when(t == 0)
    def _():
        for pp in range(N_PAIR):
            s_bd = jnp.concatenate([s0_ref[pair(pp), :]] * HEADS_PER_PAIR, axis=1) * bd
            st_ref[pp] = s_bd.T
        ubuf_ref[0:HALO - (CONV_W - 1), :] = jnp.zeros((HALO - (CONV_W - 1), C_CONV), F32)
        ubuf_ref[HALO - (CONV_W - 1):HALO, :] = cbuf_ref[...]

    @pl.when(t > 0)
    def _():
        ubuf_ref[0:HALO, :] = ubuf_ref[tm:tm + HALO, :]

    lg = lbl_ref[...]
    e = jnp.exp(lg - lg.max(axis=0, keepdims=True))
    p = e / e.sum(axis=0, keepdims=True)
    lb = jnp.zeros((1, hw), F32)
    for m in range(1, layer + 1):
        lb = lb + p[m:m + 1, :]

    qraw = pbc_ref[:, 0:hw]
    z = pbc_ref[:, hw:2 * hw]
    qs_ref[...] = qraw * _sigmoid(qraw)
    log_sig = -(jnp.maximum(-z, 0.0) + jnp.log(1.0 + jnp.exp(-jnp.abs(z))))
    a1 = jnp.log(lb)
    b1 = jnp.log1p(-lb) + log_sig
    log_f = jnp.maximum(a1, b1) + jnp.log(1.0 + jnp.exp(-jnp.abs(a1 - b1)))
    kk_ref[...] = (1.0 - lb) * _sigmoid(-z)

    tr = lax.broadcasted_iota(I32, (2 * tm, tm), 0)
    tc = lax.broadcasted_iota(I32, (2 * tm, tm), 1)
    tr_tok = jnp.where(tr < tm, tr, tr - tm)
    ones_mat = jnp.where((tr_tok // SUB) == (tc // SUB), jnp.where(tr < tm, jnp.where(tc <= tr_tok, 1.0, 0.0), 1.0), 0.0)
    ones_mat = ones_mat.astype(BF16)
    cum = sum(jnp.dot(ones_mat, part, preferred_element_type=F32) for part in _split3(log_f))
    a_loc, a_end = cum[0:tm], cum[tm:2 * tm]
    a_ref[...] = a_loc
    dd_ref[...] = jnp.exp(a_end)
    qd = (qs_ref[...] * jnp.exp(a_loc)).astype(BF16)
    kd = (kk_ref[...] * jnp.exp(a_end - a_loc)).astype(BF16)
    for s in range(n_sub):
        for pp in range(N_PAIR):
            blk = (pp, slice(s * SUB, (s + 1) * SUB), slice(s * LANES, (s + 1) * LANES))
            qmask_ref[blk] = qd[s * SUB:(s + 1) * SUB, pair(pp)]
            kmask_ref[blk] = kd[s * SUB:(s + 1) * SUB, pair(pp)]

    ii = lax.broadcasted_iota(I32, (SUB, hw), 0)

    def build_x(s, carry):
        r0 = pl.multiple_of(s * SUB, SUB)
        a_s = a_ref[pl.ds(r0, SUB), :]
        q_s = qs_ref[pl.ds(r0, SUB), :]
        k_s = kk_ref[pl.ds(r0, SUB), :]
        for j in range(SUB):
            dec = jnp.exp(jnp.where(ii >= j, a_s - a_s[j:j + 1], -jnp.inf))
            xbuf_ref[s, j * SUB:(j + 1) * SUB, :] = (q_s * k_s[j:j + 1] * dec).astype(BF16)
        return carry

    lax.fori_loop(0, n_sub, build_x, 0)
    for pp in range(N_PAIR):
        x = xbuf_ref[:, :, pair(pp)].reshape(n_sub * SUB * SUB, LANES)
        r_ref[:, :, pair(pp)] = jnp.dot(x, bd16, preferred_element_type=F32).reshape(n_sub, SUB * SUB, LANES)

    def intra(s, carry):
        r0 = pl.multiple_of(s * SUB, SUB)
        v_s = pbc_ref[pl.ds(r0, SUB), 2 * hw:3 * hw]
        o = jnp.zeros((SUB, hw), F32)
        for j in range(SUB):
            o = o + r_ref[s, j * SUB:(j + 1) * SUB, :] * v_s[j:j + 1]
        oi_ref[pl.ds(r0, SUB), :] = o
        return carry

    lax.fori_loop(0, n_sub, intra, 0)

    own_lanes = [jnp.where(lax.broadcasted_iota(I32, (DV_B, LANES), 1) // DK_B == hh, 1.0, 0.0)
                 for hh in range(HEADS_PER_PAIR)]
    o_pairs = []
    for pp in range(N_PAIR):
        val = pbc_ref[:, 2 * hw + pp * LANES:2 * hw + (pp + 1) * LANES].astype(BF16)
        ut_ref[pp] = lax.dot_general(val, kmask_ref[pp], TN_DIMS, preferred_element_type=F32)
        slabs = [st_ref[pp, hh * DV_B:(hh + 1) * DV_B, :] for hh in range(HEADS_PER_PAIR)]
        for s in range(n_sub):
            for hh in range(HEADS_PER_PAIR):
                blk = (pp, slice(hh * DV_B, (hh + 1) * DV_B), slice(s * LANES, (s + 1) * LANES))
                stk_ref[blk] = slabs[hh].astype(BF16)
                slabs[hh] = slabs[hh] * dd_ref[s * SUB:s * SUB + 1, pair(pp)] + ut_ref[blk] * own_lanes[hh]
        for hh in range(HEADS_PER_PAIR):
            st_ref[pp, hh * DV_B:(hh + 1) * DV_B, :] = slabs[hh]
        o_pairs.append(lax.dot_general(qmask_ref[pp], stk_ref[pp], NT_DIMS, preferred_element_type=F32))
    o = oi_ref[...] + jnp.concatenate(o_pairs, axis=1)

    sq = (o * o).astype(BF16)
    ms = jnp.concatenate([jnp.dot(sq[:, pair(pp)], bd16, preferred_element_type=F32) for pp in range(N_PAIR)], axis=1)
    ms = ms * (1.0 / DV_B)
    gate = pbc_ref[:, 3 * hw:4 * hw]
    obc_ref[:, 0:WIDTH_B] = o * lax.rsqrt(ms + EPS) * gn_ref[...] * (gate * _sigmoid(gate))

    c0 = 4 * hw
    u = pbc_ref[:, c0:c0 + C_CONV] * _sigmoid(pbc_ref[:, c0 + C_CONV:c0 + 2 * C_CONV])
    ubuf_ref[HALO:HALO + tm, :] = u
    y = jnp.zeros((tm, C_CONV), F32)
    first = HALO - (CONV_W - 1)
    for r in range(SUBLANES):
        taps = range(r, CONV_W, SUBLANES)
        win = ubuf_ref[pl.ds(first + r, tm + SUBLANES * (len(taps) - 1)), :]
        for n, w in enumerate(taps):
            y = y + win[SUBLANES * n:SUBLANES * n + tm] * cw_ref[w:w + 1, :]
    y = y + cb_ref[...]
    yc = y - jnp.mean(y, axis=-1, keepdims=True)
    ln = yc * lax.rsqrt(jnp.mean(yc * yc, axis=-1, keepdims=True) + EPS) * lng_ref[...] + lnb_ref[...]
    obc_ref[:, WIDTH_B:WIDTH_BC] = ln * _sigmoid(ln)

    @pl.when(t == pl.num_programs(1) - 1)
    def _():
        for pp in range(N_PAIR):
            s_bd = st_ref[pp].T
            s_new = s_bd[:, 0:DV_B]
            for hh in range(1, HEADS_PER_PAIR):
                s_new = s_new + s_bd[:, hh * DV_B:(hh + 1) * DV_B]
            sout_ref[pair(pp), :] = s_new
        cout_ref[...] = ubuf_ref[tm + HALO - (CONV_W - 1):tm + HALO, :]


def _hgrn_conv(pbc, lb_logits, gn, cw, cb, lng, lnb, s0, cbuf, *, layer, seq, tm):
    b = s0.shape[0]
    n_t = seq // tm
    n_sub = tm // SUB
    hw = H_B * DK_B
    body = functools.partial(_hgrn_conv_body, layer=layer, tm=tm)
    per_batch = lambda bi, ti: (bi, 0, 0)
    tok = lambda bi, ti: (bi * n_t + ti, 0)
    return pl.pallas_call(
        body,
        grid=(b, n_t),
        in_specs=[pl.BlockSpec((tm, BC_W), tok),
                  _resident((DEPTH, hw)),
                  _resident((1, hw)),
                  _resident((CONV_W, C_CONV)),
                  _resident((1, C_CONV)),
                  _resident((1, C_CONV)),
                  _resident((1, C_CONV)),
                  pl.BlockSpec((None, hw, DV_B), per_batch),
                  pl.BlockSpec((None, CONV_W - 1, C_CONV), per_batch)],
        out_specs=[pl.BlockSpec((tm, WIDTH_BC), tok),
                   pl.BlockSpec((None, hw, DV_B), per_batch),
                   pl.BlockSpec((None, CONV_W - 1, C_CONV), per_batch)],
        out_shape=[jax.ShapeDtypeStruct((b * seq, WIDTH_BC), F32),
                   jax.ShapeDtypeStruct((b, hw, DV_B), F32),
                   jax.ShapeDtypeStruct((b, CONV_W - 1, C_CONV), F32)],
        scratch_shapes=[pltpu.VMEM((N_PAIR, LANES, LANES), F32),
                        pltpu.VMEM((tm + HALO, C_CONV), F32),
                        pltpu.VMEM((n_sub, SUB * SUB, hw), BF16),
                        pltpu.VMEM((n_sub, SUB * SUB, hw), F32),
                        pltpu.VMEM((tm, hw), F32),
                        pltpu.VMEM((tm, hw), F32),
                        pltpu.VMEM((tm, hw), F32),
                        pltpu.VMEM((tm, hw), F32),
                        pltpu.VMEM((N_PAIR, tm, n_sub * LANES), BF16),
                        pltpu.VMEM((N_PAIR, tm, n_sub * LANES), BF16),
                        pltpu.VMEM((N_PAIR, LANES, n_sub * LANES), BF16),
                        pltpu.VMEM((N_PAIR, LANES, n_sub * LANES), F32),
                        pltpu.VMEM((tm, hw), F32)],
        compiler_params=_params("arbitrary", "arbitrary"),
        name="hgrn_conv",
    )(pbc, lb_logits, gn, cw, cb, lng, lnb, s0, cbuf)


def _outproj_body(x_ref, oa_ref, obc_ref, wa_ref, wbc_ref, g_ref, o_ref):
    m = jnp.dot(oa_ref[...].astype(BF16), wa_ref[...], preferred_element_type=F32)
    m = m + jnp.dot(obc_ref[...].astype(BF16), wbc_ref[...], preferred_element_type=F32)
    o_ref[...] = x_ref[...] + _rms(m, g_ref[...])


def _out_proj(x, oa, obc, wa, wbc, g, tm):
    t = x.shape[0]
    row = lambda i: (i, 0)
    return pl.pallas_call(
        _outproj_body,
        grid=(t // tm,),
        in_specs=[pl.BlockSpec((tm, D_MODEL), row),
                  pl.BlockSpec((tm, WIDTH_A), row),
                  pl.BlockSpec((tm, WIDTH_BC), row),
                  _resident((WIDTH_A, D_MODEL)),
                  _resident((WIDTH_BC, D_MODEL)),
                  _resident((1, D_MODEL))],
        out_specs=pl.BlockSpec((tm, D_MODEL), row),
        out_shape=jax.ShapeDtypeStruct((t, D_MODEL), F32),
        compiler_params=_params("arbitrary"),
        name="out_proj",
    )(x, oa, obc, wa, wbc, g)


def _split_w_in(w_in):
    sizes = (WIDTH_A, KV_W, KV_W, IQ_W, D_IDX, H_IDX)
    offs = [0]
    for s in sizes:
        offs.append(offs[-1] + s)
    wb = w_in.astype(BF16)
    q, k, v, qi, ki, wi = (wb[:, offs[n]:offs[n + 1]] for n in range(len(sizes)))
    return {"q_t": q.T, "qi_t": qi.T, "wi_t": wi.T, "k": k, "v": v, "ki": ki, "bc": wb[:, offs[-1]:]}


def _layer(x, past, lw, bias, *, layer, batch, seq, tm_tok, tm_seq):
    x = _ffn_half(x, lw["norm_ffn1"], lw["ffn1_gu"], lw["ffn1_down"], tm_tok)
    qt, qit, wit, k_new, v_new, ki_new, pbc = _in_proj(x, lw["norm_mix"][0:1], lw["w_in"], tm_tok)
    hw = H_B * DK_B
    if past is None:
        assert seq == _dsa_key_rows(seq // QB, 0)
        o_a = _dsa(qt, qit, wit, k_new.reshape(batch, seq, KV_W), v_new.reshape(batch, seq, KV_W),
                   ki_new.reshape(batch, seq, D_IDX), bias, n_qb=seq // QB, q_base=0, tile_base=0, l_valid=seq)
        s0 = jnp.zeros((batch, hw, DV_B), F32)
        cbuf = jnp.zeros((batch, CONV_W - 1, C_CONV), F32)
    else:
        k_c, v_c, ki_c, s0, cbuf = past
        past_len = k_c.shape[1]
        assert past_len % QB == 0 and seq <= QB
        l_valid = past_len + seq
        rows = _dsa_key_rows(1, past_len // QB)
        pad_k = lambda c, n, w: jnp.pad(jnp.concatenate([c.reshape(batch, past_len, w), n.reshape(batch, seq, w)], axis=1),
                                        ((0, 0), (0, rows - l_valid), (0, 0)))
        pad_q = lambda a: jnp.pad(a.reshape(a.shape[0], batch, seq), ((0, 0), (0, 0), (0, QB - seq))).reshape(a.shape[0], batch * QB)
        o_a = _dsa(pad_q(qt), pad_q(qit), pad_q(wit), pad_k(k_c, k_new, KV_W), pad_k(v_c, v_new, KV_W),
                   pad_k(ki_c, ki_new, D_IDX), bias, n_qb=1, q_base=past_len, tile_base=past_len // QB, l_valid=l_valid)
        o_a = o_a.reshape(batch, QB, WIDTH_A)[:, :seq].reshape(batch * seq, WIDTH_A)
        s0 = s0.reshape(batch, hw, DV_B)
    obc, s_new, c_new = _hgrn_conv(pbc, lw["lb_logits"], lw["gnorm"], lw["conv_w"], lw["conv_b"], lw["ln_g"], lw["ln_b"],
                                   s0, cbuf, layer=layer, seq=seq, tm=tm_seq)
    x = _out_proj(x, o_a, obc, lw["w_out_a"], lw["w_out_bc"], lw["norm_mix"][1:2], tm_tok)
    x = _ffn_half(x, lw["norm_ffn2"], lw["ffn2_gu"], lw["ffn2_down"], tm_tok)
    state = (k_new.reshape(batch, seq, N_KV, HEAD_DIM), v_new.reshape(batch, seq, N_KV, HEAD_DIM),
             ki_new.reshape(batch, seq, D_IDX), s_new.reshape(batch, H_B, DK_B, DV_B), c_new)
    return x, state


def kernel(x_prompt, x_sample, cache_attn_k, cache_attn_v, cache_idx_k, state_hgrn, state_conv, norm_ffn1, ffn1_w_gate_up, ffn1_w_down, norm_mix, w_in, w_out, rel_bias, hgrn_lb_logits, hgrn_gnorm, conv_w, conv_b, conv_ln_g, conv_ln_b, norm_ffn2, ffn2_w_gate_up, ffn2_w_down):
    bp, sp, _ = x_prompt.shape
    bs, ss, _ = x_sample.shape
    bias = _bias_tiles(rel_bias)
    xp = x_prompt.reshape(bp * sp, D_MODEL)
    xs = x_sample.reshape(bs * ss, D_MODEL)
    st_p, st_s = [], []
    for l in range(DEPTH):
        lw = {
            "norm_ffn1": norm_ffn1[l], "ffn1_gu": ffn1_w_gate_up[l].astype(BF16), "ffn1_down": ffn1_w_down[l].astype(BF16),
            "norm_mix": norm_mix[l], "w_in": _split_w_in(w_in[l]),
            "w_out_a": w_out[l, :WIDTH_A].astype(BF16), "w_out_bc": w_out[l, WIDTH_A:].astype(BF16),
            "lb_logits": hgrn_lb_logits, "gnorm": jnp.tile(hgrn_gnorm[l], H_B)[None, :],
            "conv_w": conv_w[l], "conv_b": conv_b[l][None, :], "ln_g": conv_ln_g[l][None, :], "ln_b": conv_ln_b[l][None, :],
            "norm_ffn2": norm_ffn2[l], "ffn2_gu": ffn2_w_gate_up[l].astype(BF16), "ffn2_down": ffn2_w_down[l].astype(BF16),
        }
        xp, sp_state = _layer(xp, None, lw, bias, layer=l, batch=bp, seq=sp, tm_tok=min(512, bp * sp), tm_seq=min(256, sp))
        past = (cache_attn_k[l], cache_attn_v[l], cache_idx_k[l], state_hgrn[l], state_conv[l])
        xs, ss_state = _layer(xs, past, lw, bias, layer=l, batch=bs, seq=ss, tm_tok=min(512, bs * ss), tm_seq=min(256, ss))
        st_p.append(sp_state)
        st_s.append(ss_state)
    stack = lambda sts, j: jnp.stack([s[j] for s in sts], axis=0)
    return (xp.reshape(bp, sp, D_MODEL), xs.reshape(bs, ss, D_MODEL),
            *(stack(st_p, j) for j in range(5)), *(stack(st_s, j) for j in range(5)))
```

```python
import functools
import math

import jax
import jax.numpy as jnp
from jax import lax
from jax.experimental import pallas as pl
from jax.experimental.pallas import tpu as pltpu

D_MODEL = 1024
DEPTH = 2
CHUNK = 64
CHUNK_SHIFT = 6
H_A = 8
HEAD_DIM = 64
N_KV = 2
GROUP = H_A // N_KV
H_IDX = 8
D_IDX = 64
TOPK_MAX = 256
NUM_BUCKETS = 32
MAX_DISTANCE = 128
H_B = 4
DK_B = 64
DV_B = 64
C_CONV = 256
CONV_W = 31
D_FF = 2816
EPS = 1e-6

WIDTH_A = H_A * HEAD_DIM
WIDTH_B = H_B * DV_B
WIDTH_BC = WIDTH_B + C_CONV
KV_W = N_KV * HEAD_DIM
IQ_W = H_IDX * D_IDX
BC_W = 4 * WIDTH_B + 2 * C_CONV

F32 = jnp.float32
BF16 = jnp.bfloat16
I32 = jnp.int32
I16 = jnp.int16

LANES = 128
SUBLANES = 8
PACK16 = 2 * SUBLANES
QB = LANES
SCORE_TILES = 2
ATT_TILES = 4
VT_ROWS = HEAD_DIM + PACK16
SUB = 16
HEADS_PER_PAIR = LANES // DK_B
N_PAIR = H_B // HEADS_PER_PAIR
HALO = 32
FFN_CHUNK = 256
VMEM_LIMIT = 56 * 1024 * 1024
INT_MIN = -2 ** 31
NEG_INF_KEY = INT_MIN + 0x7FFFFF
I16_MIN = -2 ** 15
I16_MAX = 2 ** 15 - 1
NEG_BIG = -1e30
MASK_NEG = -2.0 ** 100
LOG2E = 1.4426950408889634

NT_DIMS = (((1,), (1,)), ((), ()))
TN_DIMS = (((0,), (0,)), ((), ()))


def _bucket_thresholds():
    half = NUM_BUCKETS // 2
    max_exact = half // 2

    def big(n):
        return min(max_exact + int(math.log(n / max_exact) / math.log(MAX_DISTANCE / max_exact) * (half - max_exact)),
                   half - 1)

    return tuple(min(n for n in range(max_exact, 4 * MAX_DISTANCE) if big(n) >= b) for b in range(max_exact + 1, half))


BUCKET_THRESHOLDS = _bucket_thresholds()


def _params(*sem):
    return pltpu.CompilerParams(dimension_semantics=sem, vmem_limit_bytes=VMEM_LIMIT)


def _resident(shape):
    return pl.BlockSpec(shape, lambda *_: (0,) * len(shape), pipeline_mode=pl.Buffered(1))


def _rms(x, g):
    return x * lax.rsqrt(jnp.mean(x * x, axis=-1, keepdims=True) + EPS) * g


def _sigmoid(x):
    return 0.5 * jnp.tanh(0.5 * x) + 0.5


def _sum_leading(x):
    parts = [x[i] for i in range(x.shape[0])]
    while len(parts) > 1:
        parts = [a + b for a, b in zip(parts[0::2], parts[1::2])] + ([parts[-1]] if len(parts) % 2 else [])
    return parts[0]


def _ffn_body(x_ref, n_ref, wgu_ref, wd_ref, o_ref, acc_ref):
    x = x_ref[...]
    h = _rms(x, n_ref[0:1, :]).astype(BF16)
    for c in range(D_FF // FFN_CHUNK):
        lo, hi = c * FFN_CHUNK, (c + 1) * FFN_CHUNK
        g = jnp.dot(h, wgu_ref[:, lo:hi], preferred_element_type=F32)
        u = jnp.dot(h, wgu_ref[:, D_FF + lo:D_FF + hi], preferred_element_type=F32)
        a = (g * _sigmoid(g) * u).astype(BF16)
        part = jnp.dot(a, wd_ref[lo:hi, :], preferred_element_type=F32)
        if c == 0:
            acc_ref[...] = part
        else:
            acc_ref[...] += part
    o_ref[...] = x + 0.5 * _rms(acc_ref[...], n_ref[1:2, :])


def _ffn_half(x, norms, w_gu, w_down, tm):
    t = x.shape[0]
    return pl.pallas_call(
        _ffn_body,
        grid=(t // tm,),
        in_specs=[pl.BlockSpec((tm, D_MODEL), lambda i: (i, 0)),
                  _resident((2, D_MODEL)),
                  _resident((D_MODEL, 2 * D_FF)),
                  _resident((D_FF, D_MODEL))],
        out_specs=pl.BlockSpec((tm, D_MODEL), lambda i: (i, 0)),
        out_shape=jax.ShapeDtypeStruct((t, D_MODEL), F32),
        scratch_shapes=[pltpu.VMEM((tm, D_MODEL), F32)],
        compiler_params=_params("arbitrary"),
        name="ffn_half",
    )(x, norms, w_gu, w_down)


def _inproj_body(x_ref, g_ref, wq_ref, wqi_ref, wwi_ref, wk_ref, wv_ref, wki_ref, wbc_ref,
                 qt_ref, qit_ref, wit_ref, k_ref, v_ref, ki_ref, pbc_ref):
    h = _rms(x_ref[...], g_ref[...]).astype(BF16)
    qt = lax.dot_general(wq_ref[...], h, NT_DIMS, preferred_element_type=F32)
    qt_ref[...] = (qt * (HEAD_DIM ** -0.5 * LOG2E)).astype(BF16)
    qit_ref[...] = lax.dot_general(wqi_ref[...], h, NT_DIMS, preferred_element_type=F32).astype(BF16)
    wit_ref[...] = lax.dot_general(wwi_ref[...], h, NT_DIMS, preferred_element_type=F32) * (H_IDX ** -0.5)
    k_ref[...] = jnp.dot(h, wk_ref[...], preferred_element_type=F32)
    v_ref[...] = jnp.dot(h, wv_ref[...], preferred_element_type=F32)
    ki_ref[...] = jnp.dot(h, wki_ref[...], preferred_element_type=F32)
    pbc_ref[...] = jnp.dot(h, wbc_ref[...], preferred_element_type=F32)


def _in_proj(x, g, w, tm):
    t = x.shape[0]
    row = lambda i: (i, 0)
    col = lambda i: (0, i)
    return pl.pallas_call(
        _inproj_body,
        grid=(t // tm,),
        in_specs=[pl.BlockSpec((tm, D_MODEL), row),
                  _resident((1, D_MODEL)),
                  _resident((WIDTH_A, D_MODEL)),
                  _resident((IQ_W, D_MODEL)),
                  _resident((H_IDX, D_MODEL)),
                  _resident((D_MODEL, KV_W)),
                  _resident((D_MODEL, KV_W)),
                  _resident((D_MODEL, D_IDX)),
                  _resident((D_MODEL, BC_W))],
        out_specs=[pl.BlockSpec((WIDTH_A, tm), col),
                   pl.BlockSpec((IQ_W, tm), col),
                   pl.BlockSpec((H_IDX, tm), col),
                   pl.BlockSpec((tm, KV_W), row),
                   pl.BlockSpec((tm, KV_W), row),
                   pl.BlockSpec((tm, D_IDX), row),
                   pl.BlockSpec((tm, BC_W), row)],
        out_shape=[jax.ShapeDtypeStruct((WIDTH_A, t), BF16),
                   jax.ShapeDtypeStruct((IQ_W, t), BF16),
                   jax.ShapeDtypeStruct((H_IDX, t), F32),
                   jax.ShapeDtypeStruct((t, KV_W), F32),
                   jax.ShapeDtypeStruct((t, KV_W), F32),
                   jax.ShapeDtypeStruct((t, D_IDX), F32),
                   jax.ShapeDtypeStruct((t, BC_W), F32)],
        compiler_params=_params("arbitrary"),
        name="in_proj",
    )(x, g, w["q_t"], w["qi_t"], w["wi_t"], w["k"], w["v"], w["ki"], w["bc"])


def _bias_body(rb_ref, o_ref):
    j = lax.broadcasted_iota(I32, (QB, QB), 0)
    i = lax.broadcasted_iota(I32, (QB, QB), 1)
    half = NUM_BUCKETS // 2
    max_exact = half // 2
    for var, off in ((0, -QB), (1, 0)):
        rel = j - i + off
        n = jnp.abs(rel)
        big = jnp.full((QB, QB), max_exact, I32)
        for thr in BUCKET_THRESHOLDS:
            big = big + jnp.where(n >= thr, 1, 0)
        bucket = jnp.where(n < max_exact, n, big) + jnp.where(rel > 0, half, 0)
        for head in range(H_A):
            g, hh = divmod(head, GROUP)
            far = rb_ref[half - 1, head]
            tile = jnp.zeros((QB, QB), F32)
            for b in range(NUM_BUCKETS):
                tile = jnp.where(bucket == b, (rb_ref[b, head] - far) * LOG2E, tile)
            o_ref[g, var, :, hh * QB:(hh + 1) * QB] = tile


def _bias_tiles(rel_bias):
    return pl.pallas_call(
        _bias_body,
        in_specs=[pl.BlockSpec(memory_space=pltpu.SMEM)],
        out_shape=jax.ShapeDtypeStruct((N_KV, 2, QB, GROUP * QB), F32),
        name="rel_bias_tiles",
    )(rel_bias)


def _dsa_body(qt_ref, qit_ref, wit_ref, k_ref, v_ref, ki_ref, bias_ref, o_ref,
              keys_ref, khi_ref, klo_ref, kb_ref, kib_ref, vt_ref, acc_ref, s_ref, *,
              q_base, tile_base, l_valid, topk, idx_bits, n_tiles_pad):
    qb = pl.program_id(1)
    nt = tile_base + qb + 1
    srows = SCORE_TILES * QB
    arows = ATT_TILES * QB

    @pl.when(qb == 0)
    def _():
        ones_rows = jnp.where(lax.broadcasted_iota(I32, (PACK16, QB), 0) == 0, 1.0, 0.0)
        for t in range(n_tiles_pad):
            rows = slice(t * QB, (t + 1) * QB)
            kt = k_ref[rows, :]
            vt = v_ref[rows, :].T
            for g in range(N_KV):
                kg = jnp.concatenate([kt[:, g * HEAD_DIM:(g + 1) * HEAD_DIM], jnp.zeros((QB, LANES - HEAD_DIM), F32)], axis=1)
                kb_ref[g, rows, :] = kg.astype(BF16)
                vt_ref[t, g] = jnp.concatenate([vt[g * HEAD_DIM:(g + 1) * HEAD_DIM, :], ones_rows], axis=0).astype(BF16)
            kib_ref[rows, :] = ki_ref[rows, :].astype(BF16)

    def admissible(r0, rows):
        k_pos = r0 + lax.broadcasted_iota(I32, (rows, QB), 0)
        q_chunk = (q_base + qb * QB + lax.broadcasted_iota(I32, (rows, QB), 1)) >> CHUNK_SHIFT
        return jnp.where(k_pos < l_valid, k_pos >> CHUNK_SHIFT, 2 ** 30) <= q_chunk

    n2 = (nt + SCORE_TILES - 1) // SCORE_TILES

    qi_pairs = [jnp.concatenate([qit_ref[h * D_IDX:(h + 1) * D_IDX, :] for h in (hp, hp + 1)], axis=1)
                for hp in range(0, H_IDX, 2)]

    def score_step(u, carry):
        r0 = pl.multiple_of(u * srows, srows)
        ki = kib_ref[pl.ds(r0, srows), :]
        sc = None
        for n, qi in enumerate(qi_pairs):
            lg = jnp.dot(ki, qi, preferred_element_type=F32)
            for hh in range(2):
                w = jnp.maximum(lg[:, hh * QB:(hh + 1) * QB], 0.0) * wit_ref[2 * n + hh:2 * n + hh + 1, :]
                sc = w if sc is None else sc + w
        sc = jnp.where(admissible(r0, srows), sc + 0.0, -jnp.inf)
        bits = pltpu.bitcast(sc, I32)
        key = bits ^ ((bits >> 31) & 0x7FFFFFFF)
        keys_ref[pl.ds(r0, srows), :] = key
        khi_ref[pl.ds(r0, srows), :] = (key >> 16).astype(I16)
        klo_ref[pl.ds(r0, srows), :] = ((key & 0xFFFF) + I16_MIN).astype(I16)
        return carry

    lax.fori_loop(0, n2, score_step, 0)

    def count32(hit_fn):
        def body(t, acc):
            r0 = pl.multiple_of(t * srows, srows)
            hit = hit_fn(keys_ref[pl.ds(r0, srows), :], r0)
            return acc + hit.reshape(srows // SUBLANES, SUBLANES, QB).sum(axis=0)

        acc = lax.fori_loop(0, n2, body, jnp.zeros((SUBLANES, QB), I32))
        return acc.sum(axis=0, keepdims=True)

    crows = 2 * srows
    n4 = (n2 + 1) // 2

    @pl.when(n2 % 2 == 1)
    def _():
        r0 = pl.multiple_of(n2 * srows, srows)
        khi_ref[pl.ds(r0, srows), :] = jnp.full((srows, QB), I16_MIN, I16)
        klo_ref[pl.ds(r0, srows), :] = jnp.full((srows, QB), I16_MIN, I16)

    def count16(ref, hit_fn):
        def body(t, acc):
            r0 = pl.multiple_of(t * crows, crows)
            hit = hit_fn(ref[pl.ds(r0, crows), :].reshape(crows // PACK16, PACK16, QB))
            return acc + _sum_leading(hit)

        acc = lax.fori_loop(0, n4, body, jnp.zeros((PACK16, QB), I16))
        return acc.astype(I32).sum(axis=0, keepdims=True)

    def bcast16(x):
        return jnp.broadcast_to(x, (PACK16, QB)).astype(I16)[None]

    one16 = jnp.ones((1, PACK16, QB), I16)
    zero16 = jnp.zeros((1, PACK16, QB), I16)

    def count16_ge(ref, cand):
        c16 = bcast16(cand)
        return count16(ref, lambda blk: jnp.where(blk >= c16, one16, zero16))

    def bisect16(count_ge, total):
        c0, need0 = count_ge(jnp.zeros((1, QB), I32))
        val = jnp.where(c0 >= need0, 0, I16_MIN)
        cnt = jnp.where(c0 >= need0, c0, total)

        def step(bi, carry):
            val, cnt = carry
            cand = val + lax.shift_left(jnp.int32(1), 14 - bi)
            c, need = count_ge(cand)
            return jnp.where(c >= need, cand, val), jnp.where(c >= need, c, cnt)

        return lax.fori_loop(0, 15, step, (val, cnt))

    t_hi, _ = bisect16(lambda cand: (count16_ge(khi_ref, cand), topk), n4 * crows)
    above = jnp.where(t_hi == I16_MAX, 0, count16_ge(khi_ref, jnp.minimum(t_hi + 1, I16_MAX)))
    hi16 = bcast16(t_hi)
    low_min = jnp.full((1, PACK16, QB), I16_MIN, I16)

    def mask_low(t, acc):
        r0 = pl.multiple_of(t * crows, crows)
        same = khi_ref[pl.ds(r0, crows), :].reshape(crows // PACK16, PACK16, QB) == hi16
        lo = klo_ref[pl.ds(r0, crows), :].reshape(crows // PACK16, PACK16, QB)
        klo_ref[pl.ds(r0, crows), :] = jnp.where(same, lo, low_min).reshape(crows, QB)
        return acc + _sum_leading(jnp.where(same, one16, zero16))

    same_hi = lax.fori_loop(0, n4, mask_low, jnp.zeros((PACK16, QB), I16)).astype(I32).sum(axis=0, keepdims=True)
    need_lo = topk - above
    t_lo, cnt_lo = bisect16(lambda cand: (count16_ge(klo_ref, cand), need_lo), same_hi)
    thr = lax.shift_left(t_hi, 16) | (t_lo - I16_MIN)
    cnt = above + cnt_lo

    excess = jnp.where(thr > NEG_INF_KEY, cnt - topk, 0)

    @pl.when(jnp.max(excess) > 0)
    def _():
        need = topk - count32(lambda blk, r0: jnp.where(blk > thr, 1, 0))
        rowi = lax.broadcasted_iota(I32, (srows, QB), 0)

        def idx_step(bi, last):
            cand = last + lax.shift_left(jnp.int32(1), idx_bits - 1 - bi)
            c = count32(lambda blk, r0: jnp.where(blk == thr, jnp.where(r0 + rowi < cand, 1, 0), 0))
            return jnp.where(c < need, cand, last)

        last = lax.fori_loop(0, idx_bits, idx_step, jnp.zeros((1, QB), I32))

        def demote(t, carry):
            r0 = pl.multiple_of(t * srows, srows)
            blk = keys_ref[pl.ds(r0, srows), :]
            keys_ref[pl.ds(r0, srows), :] = jnp.where(blk == thr, jnp.where(r0 + rowi > last, thr - 1, blk), blk)
            return carry

        lax.fori_loop(0, n2, demote, 0)

    eye = jnp.where(lax.broadcasted_iota(I32, (QB, QB), 0) == lax.broadcasted_iota(I32, (QB, QB), 1), 1.0, 0.0)
    eye = jnp.concatenate([eye.astype(BF16)] * GROUP, axis=1)
    q_aug = []
    for g in range(N_KV):
        q_cat = jnp.concatenate([qt_ref[(g * GROUP + hh) * HEAD_DIM:(g * GROUP + hh + 1) * HEAD_DIM, :]
                                 for hh in range(GROUP)], axis=1)
        q_aug.append(jnp.concatenate([eye, q_cat, jnp.zeros((LANES - HEAD_DIM, GROUP * QB), BF16)], axis=0))
    acc_ref[...] = jnp.zeros(acc_ref.shape, F32)

    def softmax_update(g, load_s, vt, m_old):
        m_new = jnp.maximum(m_old, load_s().max(axis=0, keepdims=True))
        p = jnp.exp2(load_s() - m_new).astype(BF16)
        acc_ref[g] = acc_ref[g] * jnp.exp2(m_old - m_new) + jnp.dot(vt, p, preferred_element_type=F32)
        return m_new

    far_rows = (nt - 2) * QB
    row_a = lax.broadcasted_iota(I32, (arows, QB), 0)

    def far_step(u, carry):
        r0 = pl.multiple_of(u * arows, arows)
        key = keys_ref[pl.ds(r0, arows), :]
        pick = jnp.where(key >= thr, jnp.where(r0 + row_a < far_rows, 0.0, MASK_NEG), MASK_NEG).astype(BF16)
        for g in range(N_KV):
            lhs = jnp.concatenate([pick, kb_ref[g, pl.ds(r0, arows), :]], axis=1)
            s_ref[g] = jnp.dot(lhs, q_aug[g], preferred_element_type=F32)
        out = []
        for g in range(N_KV):
            vt = jnp.concatenate([vt_ref[u * ATT_TILES + i, g] for i in range(ATT_TILES)], axis=1)
            out.append(softmax_update(g, lambda g=g: s_ref[g], vt, carry[g]))
        return tuple(out)

    m_far = lax.fori_loop(0, (nt + 1) // ATT_TILES, far_step, (jnp.full((1, GROUP * QB), NEG_BIG, F32),) * N_KV)

    near = (jnp.maximum(nt - 2, 0), nt - 1)

    def near_pick(tile):
        r0 = pl.multiple_of(tile * QB, QB)
        key = keys_ref[pl.ds(r0, QB), :]
        return jnp.where(key >= thr, jnp.where(admissible(r0, QB), 0.0, MASK_NEG), MASK_NEG)

    pick = jnp.concatenate([jnp.where(nt >= 2, near_pick(near[0]), MASK_NEG), near_pick(near[1])], axis=0).astype(BF16)
    for g in range(N_KV):
        kk = jnp.concatenate([kb_ref[g, pl.ds(pl.multiple_of(tl * QB, QB), QB), :] for tl in near], axis=0)
        bias = jnp.concatenate([bias_ref[g, 0], bias_ref[g, 1]], axis=0)
        s = jnp.dot(jnp.concatenate([pick, kk], axis=1), q_aug[g], preferred_element_type=F32) + bias
        vt = jnp.concatenate([vt_ref[tl, g] for tl in near], axis=1)
        softmax_update(g, lambda s=s: s, vt, m_far[g])

    heads = []
    for g in range(N_KV):
        og = acc_ref[g, 0:HEAD_DIM, :] / acc_ref[g, HEAD_DIM:HEAD_DIM + 1, :]
        heads += [og[:, hh * QB:(hh + 1) * QB] for hh in range(GROUP)]
    o_ref[...] = jnp.concatenate(heads, axis=0).T


def _dsa_key_rows(n_qb, tile_base):
    step = 2 * SCORE_TILES
    return -(-(tile_base + n_qb) // step) * step * QB


def _dsa(qt, qit, wit, k, v, ki, bias, *, n_qb, q_base, tile_base, l_valid):
    b, lk, _ = k.shape
    assert lk == _dsa_key_rows(n_qb, tile_base)
    n_tiles_pad = lk // QB
    topk = min(TOPK_MAX, l_valid // 4)
    body = functools.partial(_dsa_body, q_base=q_base, tile_base=tile_base, l_valid=l_valid, topk=topk,
                             idx_bits=max(1, (lk - 1).bit_length()), n_tiles_pad=n_tiles_pad)
    tok = lambda bi, qi: (0, bi * n_qb + qi)
    per_batch = lambda bi, qi: (bi, 0, 0)
    return pl.pallas_call(
        body,
        grid=(b, n_qb),
        in_specs=[pl.BlockSpec((WIDTH_A, QB), tok),
                  pl.BlockSpec((IQ_W, QB), tok),
                  pl.BlockSpec((H_IDX, QB), tok),
                  pl.BlockSpec((None, lk, KV_W), per_batch),
                  pl.BlockSpec((None, lk, KV_W), per_batch),
                  pl.BlockSpec((None, lk, D_IDX), per_batch),
                  _resident((N_KV, 2, QB, GROUP * QB))],
        out_specs=pl.BlockSpec((QB, WIDTH_A), lambda bi, qi: (bi * n_qb + qi, 0)),
        out_shape=jax.ShapeDtypeStruct((b * n_qb * QB, WIDTH_A), F32),
        scratch_shapes=[pltpu.VMEM((lk, QB), I32),
                        pltpu.VMEM((lk, QB), I16),
                        pltpu.VMEM((lk, QB), I16),
                        pltpu.VMEM((N_KV, lk, LANES), BF16),
                        pltpu.VMEM((lk, D_IDX), BF16),
                        pltpu.VMEM((n_tiles_pad, N_KV, VT_ROWS, QB), BF16),
                        pltpu.VMEM((N_KV, VT_ROWS, GROUP * QB), F32),
                        pltpu.VMEM((N_KV, ATT_TILES * QB, GROUP * QB), F32)],
        compiler_params=_params("arbitrary", "arbitrary"),
        name="dsa_attention",
    )(qt, qit, wit, k, v, ki, bias)


def _split3(x):
    hi = x.astype(BF16)
    r1 = x - hi.astype(F32)
    mid = r1.astype(BF16)
    return hi, mid, (r1 - mid.astype(F32)).astype(BF16)


def _hgrn_conv_body(pbc_ref, lbl_ref, gn_ref, cw_ref, cb_ref, lng_ref, lnb_ref, s0_ref, cbuf_ref,
                    obc_ref, sout_ref, cout_ref,
                    st_ref, ubuf_ref, xbuf_ref, r_ref, a_ref, qs_ref, kk_ref, dd_ref,
                    kmask_ref, ut_ref, oi_ref, *, layer, tm):
    t = pl.program_id(1)
    n_sub = tm // SUB
    hw = H_B * DK_B
    ri = lax.broadcasted_iota(I32, (LANES, LANES), 0)
    ci = lax.broadcasted_iota(I32, (LANES, LANES), 1)
    bd = jnp.where((ri // DK_B) == (ci // DK_B), 1.0, 0.0)
    bd16 = bd.astype(BF16)

    def pair(pp):
        return slice(pp * LANES, (pp + 1) * LANES)

    @pl.when((pl.program_id(0) == 0) & (t == 0))
    def _():
        kmask_ref[...] = jnp.zeros(kmask_ref.shape, BF16)

    @pl.when(t == 0)
    def _():
        for pp in range(N_PAIR):
            s_bd = jnp.concatenate([s0_ref[pair(pp), :]] * HEADS_PER_PAIR, axis=1) * bd
            st_ref[pp] = s_bd.T
        ubuf_ref[0:HALO - (CONV_W - 1), :] = jnp.zeros((HALO - (CONV_W - 1), C_CONV), F32)
        ubuf_ref[HALO - (CONV_W - 1):HALO, :] = cbuf_ref[...]

    @pl.when(t > 0)
    def _():
        ubuf_ref[0:HALO, :] = ubuf_ref[tm:tm + HALO, :]

    lg = lbl_ref[...]
    e = jnp.exp(lg - lg.max(axis=0, keepdims=True))
    p = e / e.sum(axis=0, keepdims=True)
    lb = jnp.zeros((1, hw), F32)
    for m in range(1, layer + 1):
        lb = lb + p[m:m + 1, :]

    qraw = pbc_ref[:, 0:hw]
    z = pbc_ref[:, hw:2 * hw]
    qs_ref[...] = qraw * _sigmoid(qraw)
    log_sig = -(jnp.maximum(-z, 0.0) + jnp.log(1.0 + jnp.exp(-jnp.abs(z))))
    a1 = jnp.log(lb)
    b1 = jnp.log1p(-lb) + log_sig
    log_f = jnp.maximum(a1, b1) + jnp.log(1.0 + jnp.exp(-jnp.abs(a1 - b1)))
    kk_ref[...] = (1.0 - lb) * _sigmoid(-z)

    tr = lax.broadcasted_iota(I32, (2 * tm, tm), 0)
    tc = lax.broadcasted_iota(I32, (2 * tm, tm), 1)
    tr_tok = jnp.where(tr < tm, tr, tr - tm)
    ones_mat = jnp.where((tr_tok // SUB) == (tc // SUB), jnp.where(tr < tm, jnp.where(tc <= tr_tok, 1.0, 0.0), 1.0), 0.0)
    ones_mat = ones_mat.astype(BF16)
    cum = sum(jnp.dot(ones_mat, part, preferred_element_type=F32) for part in _split3(log_f))
    a_loc, a_end = cum[0:tm], cum[tm:2 * tm]
    a_ref[...] = a_loc
    dd_ref[...] = jnp.exp(a_end)
    qd = (qs_ref[...] * jnp.exp(a_loc)).astype(BF16)
    kd = (kk_ref[...] * jnp.exp(a_end - a_loc)).astype(BF16)
    for s in range(n_sub):
        for pp in range(N_PAIR):
            kmask_ref[pp, s * SUB:(s + 1) * SUB, s * LANES:(s + 1) * LANES] = kd[s * SUB:(s + 1) * SUB, pair(pp)]

    ii = lax.broadcasted_iota(I32, (SUB, hw), 0)

    def build_x(s, carry):
        r0 = pl.multiple_of(s * SUB, SUB)
        a_s = a_ref[pl.ds(r0, SUB), :]
        q_s = qs_ref[pl.ds(r0, SUB), :]
        k_s = kk_ref[pl.ds(r0, SUB), :]
        for j in range(SUB):
            dec = jnp.exp(jnp.where(ii >= j, a_s - a_s[j:j + 1], -jnp.inf))
            xbuf_ref[s, j * SUB:(j + 1) * SUB, :] = (q_s * k_s[j:j + 1] * dec).astype(BF16)
        return carry

    lax.fori_loop(0, n_sub, build_x, 0)
    for pp in range(N_PAIR):
        x = xbuf_ref[:, :, pair(pp)].reshape(n_sub * SUB * SUB, LANES)
        r_ref[:, :, pair(pp)] = jnp.dot(x, bd16, preferred_element_type=F32).reshape(n_sub, SUB * SUB, LANES)

    def intra(s, carry):
        r0 = pl.multiple_of(s * SUB, SUB)
        v_s = pbc_ref[pl.ds(r0, SUB), 2 * hw:3 * hw]
        o = jnp.zeros((SUB, hw), F32)
        for j in range(SUB):
            o = o + r_ref[s, j * SUB:(j + 1) * SUB, :] * v_s[j:j + 1]
        oi_ref[pl.ds(r0, SUB), :] = o
        return carry

    lax.fori_loop(0, n_sub, intra, 0)

    own_lanes = [jnp.where(lax.broadcasted_iota(I32, (DV_B, LANES), 1) // DK_B == hh, 1.0, 0.0)
                 for hh in range(HEADS_PER_PAIR)]
    o_pairs = []
    for pp in range(N_PAIR):
        val = pbc_ref[:, 2 * hw + pp * LANES:2 * hw + (pp + 1) * LANES].astype(BF16)
        ut_ref[pp] = lax.dot_general(val, kmask_ref[pp], TN_DIMS, preferred_element_type=F32)
        slabs = [st_ref[pp, hh * DV_B:(hh + 1) * DV_B, :] for hh in range(HEADS_PER_PAIR)]
        reads = []
        for s in range(n_sub):
            st16 = jnp.concatenate(slabs, axis=0).astype(BF16)
            reads.append(lax.dot_general(qd[s * SUB:(s + 1) * SUB, pair(pp)], st16, NT_DIMS, preferred_element_type=F32))
            for hh in range(HEADS_PER_PAIR):
                blk = (pp, slice(hh * DV_B, (hh + 1) * DV_B), slice(s * LANES, (s + 1) * LANES))
                slabs[hh] = slabs[hh] * dd_ref[s * SUB:s * SUB + 1, pair(pp)] + ut_ref[blk] * own_lanes[hh]
        for hh in range(HEADS_PER_PAIR):
            st_ref[pp, hh * DV_B:(hh + 1) * DV_B, :] = slabs[hh]
        o_pairs.append(jnp.concatenate(reads, axis=0))
    o = oi_ref[...] + jnp.concatenate(o_pairs, axis=1)

    sq = (o * o).astype(BF16)
    ms = jnp.concatenate([jnp.dot(sq[:, pair(pp)], bd16, preferred_element_type=F32) for pp in range(N_PAIR)], axis=1)
    ms = ms * (1.0 / DV_B)
    gate = pbc_ref[:, 3 * hw:4 * hw]
    obc_ref[:, 0:WIDTH_B] = o * lax.rsqrt(ms + EPS) * gn_ref[...] * (gate * _sigmoid(gate))

    c0 = 4 * hw
    u = pbc_ref[:, c0:c0 + C_CONV] * _sigmoid(pbc_ref[:, c0 + C_CONV:c0 + 2 * C_CONV])
    ubuf_ref[HALO:HALO + tm, :] = u
    y = jnp.zeros((tm, C_CONV), F32)
    first = HALO - (CONV_W - 1)
    for r in range(SUBLANES):
        taps = range(r, CONV_W, SUBLANES)
        win = ubuf_ref[pl.ds(first + r, tm + SUBLANES * (len(taps) - 1)), :]
        for n, w in enumerate(taps):
            y = y + win[SUBLANES * n:SUBLANES * n + tm] * cw_ref[w:w + 1, :]
    y = y + cb_ref[...]
    yc = y - jnp.mean(y, axis=-1, keepdims=True)
    ln = yc * lax.rsqrt(jnp.mean(yc * yc, axis=-1, keepdims=True) + EPS) * lng_ref[...] + lnb_ref[...]
    obc_ref[:, WIDTH_B:WIDTH_BC] = ln * _sigmoid(ln)

    @pl.when(t == pl.num_programs(1) - 1)
    def _():
        for pp in range(N_PAIR):
            s_bd = st_ref[pp].T
            s_new = s_bd[:, 0:DV_B]
            for hh in range(1, HEADS_PER_PAIR):
                s_new = s_new + s_bd[:, hh * DV_B:(hh + 1) * DV_B]
            sout_ref[pair(pp), :] = s_new
        cout_ref[...] = ubuf_ref[tm + HALO - (CONV_W - 1):tm + HALO, :]


def _hgrn_conv(pbc, lb_logits, gn, cw, cb, lng, lnb, s0, cbuf, *, layer, seq, tm):
    b = s0.shape[0]
    n_t = seq // tm
    n_sub = tm // SUB
    hw = H_B * DK_B
    body = functools.partial(_hgrn_conv_body, layer=layer, tm=tm)
    per_batch = lambda bi, ti: (bi, 0, 0)
    tok = lambda bi, ti: (bi * n_t + ti, 0)
    return pl.pallas_call(
        body,
        grid=(b, n_t),
        in_specs=[pl.BlockSpec((tm, BC_W), tok),
                  _resident((DEPTH, hw)),
                  _resident((1, hw)),
                  _resident((CONV_W, C_CONV)),
                  _resident((1, C_CONV)),
                  _resident((1, C_CONV)),
                  _resident((1, C_CONV)),
                  pl.BlockSpec((None, hw, DV_B), per_batch),
                  pl.BlockSpec((None, CONV_W - 1, C_CONV), per_batch)],
        out_specs=[pl.BlockSpec((tm, WIDTH_BC), tok),
                   pl.BlockSpec((None, hw, DV_B), per_batch),
                   pl.BlockSpec((None, CONV_W - 1, C_CONV), per_batch)],
        out_shape=[jax.ShapeDtypeStruct((b * seq, WIDTH_BC), F32),
                   jax.ShapeDtypeStruct((b, hw, DV_B), F32),
                   jax.ShapeDtypeStruct((b, CONV_W - 1, C_CONV), F32)],
        scratch_shapes=[pltpu.VMEM((N_PAIR, LANES, LANES), F32),
                        pltpu.VMEM((tm + HALO, C_CONV), F32),
                        pltpu.VMEM((n_sub, SUB * SUB, hw), BF16),
                        pltpu.VMEM((n_sub, SUB * SUB, hw), F32),
                        pltpu.VMEM((tm, hw), F32),
                        pltpu.VMEM((tm, hw), F32),
                        pltpu.VMEM((tm, hw), F32),
                        pltpu.VMEM((tm, hw), F32),
                        pltpu.VMEM((N_PAIR, tm, n_sub * LANES), BF16),
                        pltpu.VMEM((N_PAIR, LANES, n_sub * LANES), F32),
                        pltpu.VMEM((tm, hw), F32)],
        compiler_params=_params("arbitrary", "arbitrary"),
        name="hgrn_conv",
    )(pbc, lb_logits, gn, cw, cb, lng, lnb, s0, cbuf)


def _outproj_body(x_ref, oa_ref, obc_ref, wa_ref, wbc_ref, g_ref, o_ref):
    m = jnp.dot(oa_ref[...].astype(BF16), wa_ref[...], preferred_element_type=F32)
    m = m + jnp.dot(obc_ref[...].astype(BF16), wbc_ref[...], preferred_element_type=F32)
    o_ref[...] = x_ref[...] + _rms(m, g_ref[...])


def _out_proj(x, oa, obc, wa, wbc, g, tm):
    t = x.shape[0]
    row = lambda i: (i, 0)
    return pl.pallas_call(
        _outproj_body,
        grid=(t // tm,),
        in_specs=[pl.BlockSpec((tm, D_MODEL), row),
                  pl.BlockSpec((tm, WIDTH_A), row),
                  pl.BlockSpec((tm, WIDTH_BC), row),
                  _resident((WIDTH_A, D_MODEL)),
                  _resident((WIDTH_BC, D_MODEL)),
                  _resident((1, D_MODEL))],
        out_specs=pl.BlockSpec((tm, D_MODEL), row),
        out_shape=jax.ShapeDtypeStruct((t, D_MODEL), F32),
        compiler_params=_params("arbitrary"),
        name="out_proj",
    )(x, oa, obc, wa, wbc, g)


def _split_w_in(w_in):
    sizes = (WIDTH_A, KV_W, KV_W, IQ_W, D_IDX, H_IDX)
    offs = [0]
    for s in sizes:
        offs.append(offs[-1] + s)
    wb = w_in.astype(BF16)
    q, k, v, qi, ki, wi = (wb[:, offs[n]:offs[n + 1]] for n in range(len(sizes)))
    return {"q_t": q.T, "qi_t": qi.T, "wi_t": wi.T, "k": k, "v": v, "ki": ki, "bc": wb[:, offs[-1]:]}


def _layer(x, past, lw, bias, *, layer, batch, seq, tm_tok, tm_seq):
    x = _ffn_half(x, lw["norm_ffn1"], lw["ffn1_gu"], lw["ffn1_down"], tm_tok)
    qt, qit, wit, k_new, v_new, ki_new, pbc = _in_proj(x, lw["norm_mix"][0:1], lw["w_in"], tm_tok)
    hw = H_B * DK_B
    if past is None:
        assert seq == _dsa_key_rows(seq // QB, 0)
        o_a = _dsa(qt, qit, wit, k_new.reshape(batch, seq, KV_W), v_new.reshape(batch, seq, KV_W),
                   ki_new.reshape(batch, seq, D_IDX), bias, n_qb=seq // QB, q_base=0, tile_base=0, l_valid=seq)
        s0 = jnp.zeros((batch, hw, DV_B), F32)
        cbuf = jnp.zeros((batch, CONV_W - 1, C_CONV), F32)
    else:
        k_c, v_c, ki_c, s0, cbuf = past
        past_len = k_c.shape[1]
        assert past_len % QB == 0 and seq <= QB
        l_valid = past_len + seq
        rows = _dsa_key_rows(1, past_len // QB)
        pad_k = lambda c, n, w: jnp.pad(jnp.concatenate([c.reshape(batch, past_len, w), n.reshape(batch, seq, w)], axis=1),
                                        ((0, 0), (0, rows - l_valid), (0, 0)))
        pad_q = lambda a: jnp.pad(a.reshape(a.shape[0], batch, seq), ((0, 0), (0, 0), (0, QB - seq))).reshape(a.shape[0], batch * QB)
        o_a = _dsa(pad_q(qt), pad_q(qit), pad_q(wit), pad_k(k_c, k_new, KV_W), pad_k(v_c, v_new, KV_W),
                   pad_k(ki_c, ki_new, D_IDX), bias, n_qb=1, q_base=past_len, tile_base=past_len // QB, l_valid=l_valid)
        o_a = o_a.reshape(batch, QB, WIDTH_A)[:, :seq].reshape(batch * seq, WIDTH_A)
        s0 = s0.reshape(batch, hw, DV_B)
    obc, s_new, c_new = _hgrn_conv(pbc, lw["lb_logits"], lw["gnorm"], lw["conv_w"], lw["conv_b"], lw["ln_g"], lw["ln_b"],
                                   s0, cbuf, layer=layer, seq=seq, tm=tm_seq)
    x = _out_proj(x, o_a, obc, lw["w_out_a"], lw["w_out_bc"], lw["norm_mix"][1:2], tm_tok)
    x = _ffn_half(x, lw["norm_ffn2"], lw["ffn2_gu"], lw["ffn2_down"], tm_tok)
    state = (k_new.reshape(batch, seq, N_KV, HEAD_DIM), v_new.reshape(batch, seq, N_KV, HEAD_DIM),
             ki_new.reshape(batch, seq, D_IDX), s_new.reshape(batch, H_B, DK_B, DV_B), c_new)
    return x, state


def kernel(x_prompt, x_sample, cache_attn_k, cache_attn_v, cache_idx_k, state_hgrn, state_conv, norm_ffn1, ffn1_w_gate_up, ffn1_w_down, norm_mix, w_in, w_out, rel_bias, hgrn_lb_logits, hgrn_gnorm, conv_w, conv_b, conv_ln_g, conv_ln_b, norm_ffn2, ffn2_w_gate_up, ffn2_w_down):
    bp, sp, _ = x_prompt.shape
    bs, ss, _ = x_sample.shape
    bias = _bias_tiles(rel_bias)
    xp = x_prompt.reshape(bp * sp, D_MODEL)
    xs = x_sample.reshape(bs * ss, D_MODEL)
    st_p, st_s = [], []
    for l in range(DEPTH):
        lw = {
            "norm_ffn1": norm_ffn1[l], "ffn1_gu": ffn1_w_gate_up[l].astype(BF16), "ffn1_down": ffn1_w_down[l].astype(BF16),
            "norm_mix": norm_mix[l], "w_in": _split_w_in(w_in[l]),
            "w_out_a": w_out[l, :WIDTH_A].astype(BF16), "w_out_bc": w_out[l, WIDTH_A:].astype(BF16),
            "lb_logits": hgrn_lb_logits, "gnorm": jnp.tile(hgrn_gnorm[l], H_B)[None, :],
            "conv_w": conv_w[l], "conv_b": conv_b[l][None, :], "ln_g": conv_ln_g[l][None, :], "ln_b": conv_ln_b[l][None, :],
            "norm_ffn2": norm_ffn2[l], "ffn2_gu": ffn2_w_gate_up[l].astype(BF16), "ffn2_down": ffn2_w_down[l].astype(BF16),
        }
        xp, sp_state = _layer(xp, None, lw, bias, layer=l, batch=bp, seq=sp, tm_tok=min(512, bp * sp), tm_seq=min(256, sp))
        past = (cache_attn_k[l], cache_attn_v[l], cache_idx_k[l], state_hgrn[l], state_conv[l])
        xs, ss_state = _layer(xs, past, lw, bias, layer=l, batch=bs, seq=ss, tm_tok=min(512, bs * ss), tm_seq=min(256, ss))
        st_p.append(sp_state)
        st_s.append(ss_state)
    stack = lambda sts, j: jnp.stack([s[j] for s in sts], axis=0)
    return (xp.reshape(bp, sp, D_MODEL), xs.reshape(bs, ss, D_MODEL),
            *(stack(st_p, j) for j in range(5)), *(stack(st_s, j) for j in range(5)))
```

```python
import functools
import math

import jax
import jax.numpy as jnp
from jax import lax
from jax.experimental import pallas as pl
from jax.experimental.pallas import tpu as pltpu

D_MODEL = 1024
DEPTH = 2
CHUNK = 64
CHUNK_SHIFT = 6
H_A = 8
HEAD_DIM = 64
N_KV = 2
GROUP = H_A // N_KV
H_IDX = 8
D_IDX = 64
TOPK_MAX = 256
NUM_BUCKETS = 32
MAX_DISTANCE = 128
H_B = 4
DK_B = 64
DV_B = 64
C_CONV = 256
CONV_W = 31
D_FF = 2816
EPS = 1e-6

WIDTH_A = H_A * HEAD_DIM
WIDTH_B = H_B * DV_B
WIDTH_BC = WIDTH_B + C_CONV
KV_W = N_KV * HEAD_DIM
IQ_W = H_IDX * D_IDX
BC_W = 4 * WIDTH_B + 2 * C_CONV

F32 = jnp.float32
BF16 = jnp.bfloat16
I32 = jnp.int32

LANES = 128
SUBLANES = 8
PACK16 = 2 * SUBLANES
QB = LANES
SCORE_TILES = 2
ATT_TILES = 4
VT_ROWS = HEAD_DIM + PACK16
SUB = 16
HEADS_PER_PAIR = LANES // DK_B
N_PAIR = H_B // HEADS_PER_PAIR
HALO = 32
FFN_CHUNK = 256
VMEM_LIMIT = 56 * 1024 * 1024
INT_MIN = -2 ** 31
NEG_INF_KEY = INT_MIN + 0x7FFFFF
NEG_BIG = -1e30
MASK_NEG = -2.0 ** 100
LOG2E = 1.4426950408889634

NT_DIMS = (((1,), (1,)), ((), ()))
TN_DIMS = (((0,), (0,)), ((), ()))


def _bucket_thresholds():
    half = NUM_BUCKETS // 2
    max_exact = half // 2

    def big(n):
        return min(max_exact + int(math.log(n / max_exact) / math.log(MAX_DISTANCE / max_exact) * (half - max_exact)),
                   half - 1)

    return tuple(min(n for n in range(max_exact, 4 * MAX_DISTANCE) if big(n) >= b) for b in range(max_exact + 1, half))


BUCKET_THRESHOLDS = _bucket_thresholds()


def _params(*sem):
    return pltpu.CompilerParams(dimension_semantics=sem, vmem_limit_bytes=VMEM_LIMIT)


def _resident(shape):
    return pl.BlockSpec(shape, lambda *_: (0,) * len(shape), pipeline_mode=pl.Buffered(1))


def _rms(x, g):
    return x * lax.rsqrt(jnp.mean(x * x, axis=-1, keepdims=True) + EPS) * g


def _sigmoid(x):
    return 0.5 * jnp.tanh(0.5 * x) + 0.5


def _ffn_body(x_ref, n_ref, wgu_ref, wd_ref, o_ref, acc_ref):
    x = x_ref[...]
    h = _rms(x, n_ref[0:1, :]).astype(BF16)
    for c in range(D_FF // FFN_CHUNK):
        lo, hi = c * FFN_CHUNK, (c + 1) * FFN_CHUNK
        g = jnp.dot(h, wgu_ref[:, lo:hi], preferred_element_type=F32)
        u = jnp.dot(h, wgu_ref[:, D_FF + lo:D_FF + hi], preferred_element_type=F32)
        a = (g * _sigmoid(g) * u).astype(BF16)
        part = jnp.dot(a, wd_ref[lo:hi, :], preferred_element_type=F32)
        if c == 0:
            acc_ref[...] = part
        else:
            acc_ref[...] += part
    o_ref[...] = x + 0.5 * _rms(acc_ref[...], n_ref[1:2, :])


def _ffn_half(x, norms, w_gu, w_down, tm):
    t = x.shape[0]
    return pl.pallas_call(
        _ffn_body,
        grid=(t // tm,),
        in_specs=[pl.BlockSpec((tm, D_MODEL), lambda i: (i, 0)),
                  _resident((2, D_MODEL)),
                  _resident((D_MODEL, 2 * D_FF)),
                  _resident((D_FF, D_MODEL))],
        out_specs=pl.BlockSpec((tm, D_MODEL), lambda i: (i, 0)),
        out_shape=jax.ShapeDtypeStruct((t, D_MODEL), F32),
        scratch_shapes=[pltpu.VMEM((tm, D_MODEL), F32)],
        compiler_params=_params("arbitrary"),
        name="ffn_half",
    )(x, norms, w_gu, w_down)


def _inproj_body(x_ref, g_ref, wq_ref, wqi_ref, wwi_ref, wk_ref, wv_ref, wki_ref, wbc_ref,
                 qt_ref, qit_ref, wit_ref, k_ref, v_ref, ki_ref, pbc_ref):
    h = _rms(x_ref[...], g_ref[...]).astype(BF16)
    qt = lax.dot_general(wq_ref[...], h, NT_DIMS, preferred_element_type=F32)
    qt_ref[...] = (qt * (HEAD_DIM ** -0.5 * LOG2E)).astype(BF16)
    qit_ref[...] = lax.dot_general(wqi_ref[...], h, NT_DIMS, preferred_element_type=F32).astype(BF16)
    wit_ref[...] = lax.dot_general(wwi_ref[...], h, NT_DIMS, preferred_element_type=F32) * (H_IDX ** -0.5)
    k_ref[...] = jnp.dot(h, wk_ref[...], preferred_element_type=F32)
    v_ref[...] = jnp.dot(h, wv_ref[...], preferred_element_type=F32)
    ki_ref[...] = jnp.dot(h, wki_ref[...], preferred_element_type=F32)
    pbc_ref[...] = jnp.dot(h, wbc_ref[...], preferred_element_type=F32)


def _in_proj(x, g, w, tm):
    t = x.shape[0]
    row = lambda i: (i, 0)
    col = lambda i: (0, i)
    return pl.pallas_call(
        _inproj_body,
        grid=(t // tm,),
        in_specs=[pl.BlockSpec((tm, D_MODEL), row),
                  _resident((1, D_MODEL)),
                  _resident((WIDTH_A, D_MODEL)),
                  _resident((IQ_W, D_MODEL)),
                  _resident((H_IDX, D_MODEL)),
                  _resident((D_MODEL, KV_W)),
                  _resident((D_MODEL, KV_W)),
                  _resident((D_MODEL, D_IDX)),
                  _resident((D_MODEL, BC_W))],
        out_specs=[pl.BlockSpec((WIDTH_A, tm), col),
                   pl.BlockSpec((IQ_W, tm), col),
                   pl.BlockSpec((H_IDX, tm), col),
                   pl.BlockSpec((tm, KV_W), row),
                   pl.BlockSpec((tm, KV_W), row),
                   pl.BlockSpec((tm, D_IDX), row),
                   pl.BlockSpec((tm, BC_W), row)],
        out_shape=[jax.ShapeDtypeStruct((WIDTH_A, t), BF16),
                   jax.ShapeDtypeStruct((IQ_W, t), BF16),
                   jax.ShapeDtypeStruct((H_IDX, t), F32),
                   jax.ShapeDtypeStruct((t, KV_W), F32),
                   jax.ShapeDtypeStruct((t, KV_W), F32),
                   jax.ShapeDtypeStruct((t, D_IDX), F32),
                   jax.ShapeDtypeStruct((t, BC_W), F32)],
        compiler_params=_params("arbitrary"),
        name="in_proj",
    )(x, g, w["q_t"], w["qi_t"], w["wi_t"], w["k"], w["v"], w["ki"], w["bc"])


def _bias_body(rb_ref, o_ref):
    j = lax.broadcasted_iota(I32, (QB, QB), 0)
    i = lax.broadcasted_iota(I32, (QB, QB), 1)
    half = NUM_BUCKETS // 2
    max_exact = half // 2
    for var, off in ((0, -QB), (1, 0)):
        rel = j - i + off
        n = jnp.abs(rel)
        big = jnp.full((QB, QB), max_exact, I32)
        for thr in BUCKET_THRESHOLDS:
            big = big + jnp.where(n >= thr, 1, 0)
        bucket = jnp.where(n < max_exact, n, big) + jnp.where(rel > 0, half, 0)
        for head in range(H_A):
            g, hh = divmod(head, GROUP)
            far = rb_ref[half - 1, head]
            tile = jnp.zeros((QB, QB), F32)
            for b in range(NUM_BUCKETS):
                tile = jnp.where(bucket == b, (rb_ref[b, head] - far) * LOG2E, tile)
            o_ref[g, var, :, hh * QB:(hh + 1) * QB] = tile


def _bias_tiles(rel_bias):
    return pl.pallas_call(
        _bias_body,
        in_specs=[pl.BlockSpec(memory_space=pltpu.SMEM)],
        out_shape=jax.ShapeDtypeStruct((N_KV, 2, QB, GROUP * QB), F32),
        name="rel_bias_tiles",
    )(rel_bias)


def _dsa_body(qt_ref, qit_ref, wit_ref, k_ref, v_ref, ki_ref, bias_ref, o_ref,
              keys_ref, kb_ref, kib_ref, vt_ref, acc_ref, s_ref, *,
              q_base, tile_base, l_valid, topk, idx_bits, n_tiles_pad):
    qb = pl.program_id(1)
    nt = tile_base + qb + 1
    srows = SCORE_TILES * QB
    arows = ATT_TILES * QB

    @pl.when(qb == 0)
    def _():
        ones_rows = jnp.where(lax.broadcasted_iota(I32, (PACK16, QB), 0) == 0, 1.0, 0.0)
        for t in range(n_tiles_pad):
            rows = slice(t * QB, (t + 1) * QB)
            kt = k_ref[rows, :]
            vt = v_ref[rows, :].T
            for g in range(N_KV):
                kg = jnp.concatenate([kt[:, g * HEAD_DIM:(g + 1) * HEAD_DIM], jnp.zeros((QB, LANES - HEAD_DIM), F32)], axis=1)
                kb_ref[g, rows, :] = kg.astype(BF16)
                vt_ref[t, g] = jnp.concatenate([vt[g * HEAD_DIM:(g + 1) * HEAD_DIM, :], ones_rows], axis=0).astype(BF16)
            kib_ref[rows, :] = ki_ref[rows, :].astype(BF16)

    def admissible(r0, rows):
        k_pos = r0 + lax.broadcasted_iota(I32, (rows, QB), 0)
        q_chunk = (q_base + qb * QB + lax.broadcasted_iota(I32, (rows, QB), 1)) >> CHUNK_SHIFT
        return jnp.where(k_pos < l_valid, k_pos >> CHUNK_SHIFT, 2 ** 30) <= q_chunk

    n2 = (nt + SCORE_TILES - 1) // SCORE_TILES

    qi_pairs = [jnp.concatenate([qit_ref[h * D_IDX:(h + 1) * D_IDX, :] for h in (hp, hp + 1)], axis=1)
                for hp in range(0, H_IDX, 2)]

    def score_step(u, carry):
        r0 = pl.multiple_of(u * srows, srows)
        ki = kib_ref[pl.ds(r0, srows), :]
        sc = None
        for n, qi in enumerate(qi_pairs):
            lg = jnp.dot(ki, qi, preferred_element_type=F32)
            for hh in range(2):
                w = jnp.maximum(lg[:, hh * QB:(hh + 1) * QB], 0.0) * wit_ref[2 * n + hh:2 * n + hh + 1, :]
                sc = w if sc is None else sc + w
        sc = jnp.where(admissible(r0, srows), sc + 0.0, -jnp.inf)
        bits = pltpu.bitcast(sc, I32)
        keys_ref[pl.ds(r0, srows), :] = bits ^ ((bits >> 31) & 0x7FFFFFFF)
        return carry

    lax.fori_loop(0, n2, score_step, 0)

    def count32(hit_fn):
        def body(t, acc):
            r0 = pl.multiple_of(t * srows, srows)
            hit = hit_fn(keys_ref[pl.ds(r0, srows), :], r0)
            return acc + hit.reshape(srows // SUBLANES, SUBLANES, QB).sum(axis=0)

        acc = lax.fori_loop(0, n2, body, jnp.zeros((SUBLANES, QB), I32))
        return acc.sum(axis=0, keepdims=True)

    def count_ge(cand):
        return count32(lambda blk, r0: jnp.where(blk >= cand, 1, 0))

    c0 = count_ge(jnp.zeros((1, QB), I32))
    thr = jnp.where(c0 >= topk, 0, INT_MIN)
    cnt = jnp.where(c0 >= topk, c0, n2 * srows)

    def bit_step(bi, carry):
        thr, cnt = carry
        cand = thr + lax.shift_left(jnp.int32(1), 30 - bi)
        c = count_ge(cand)
        return jnp.where(c >= topk, cand, thr), jnp.where(c >= topk, c, cnt)

    thr, cnt = lax.fori_loop(0, 31, bit_step, (thr, cnt))

    excess = jnp.where(thr > NEG_INF_KEY, cnt - topk, 0)

    @pl.when(jnp.max(excess) > 0)
    def _():
        need = topk - count32(lambda blk, r0: jnp.where(blk > thr, 1, 0))
        rowi = lax.broadcasted_iota(I32, (srows, QB), 0)

        def idx_step(bi, last):
            cand = last + lax.shift_left(jnp.int32(1), idx_bits - 1 - bi)
            c = count32(lambda blk, r0: jnp.where(blk == thr, jnp.where(r0 + rowi < cand, 1, 0), 0))
            return jnp.where(c < need, cand, last)

        last = lax.fori_loop(0, idx_bits, idx_step, jnp.zeros((1, QB), I32))

        def demote(t, carry):
            r0 = pl.multiple_of(t * srows, srows)
            blk = keys_ref[pl.ds(r0, srows), :]
            keys_ref[pl.ds(r0, srows), :] = jnp.where(blk == thr, jnp.where(r0 + rowi > last, thr - 1, blk), blk)
            return carry

        lax.fori_loop(0, n2, demote, 0)

    eye = jnp.where(lax.broadcasted_iota(I32, (QB, QB), 0) == lax.broadcasted_iota(I32, (QB, QB), 1), 1.0, 0.0)
    eye = jnp.concatenate([eye.astype(BF16)] * GROUP, axis=1)
    q_aug = []
    for g in range(N_KV):
        q_cat = jnp.concatenate([qt_ref[(g * GROUP + hh) * HEAD_DIM:(g * GROUP + hh + 1) * HEAD_DIM, :]
                                 for hh in range(GROUP)], axis=1)
        q_aug.append(jnp.concatenate([eye, q_cat, jnp.zeros((LANES - HEAD_DIM, GROUP * QB), BF16)], axis=0))
    acc_ref[...] = jnp.zeros(acc_ref.shape, F32)

    def softmax_update(g, load_s, vt, m_old):
        m_new = jnp.maximum(m_old, load_s().max(axis=0, keepdims=True))
        p = jnp.exp2(load_s() - m_new).astype(BF16)
        acc_ref[g] = acc_ref[g] * jnp.exp2(m_old - m_new) + jnp.dot(vt, p, preferred_element_type=F32)
        return m_new

    far_rows = (nt - 2) * QB
    row_a = lax.broadcasted_iota(I32, (arows, QB), 0)

    def far_step(u, carry):
        r0 = pl.multiple_of(u * arows, arows)
        key = keys_ref[pl.ds(r0, arows), :]
        pick = jnp.where(key >= thr, jnp.where(r0 + row_a < far_rows, 0.0, MASK_NEG), MASK_NEG).astype(BF16)
        for g in range(N_KV):
            lhs = jnp.concatenate([pick, kb_ref[g, pl.ds(r0, arows), :]], axis=1)
            s_ref[g] = jnp.dot(lhs, q_aug[g], preferred_element_type=F32)
        out = []
        for g in range(N_KV):
            vt = jnp.concatenate([vt_ref[u * ATT_TILES + i, g] for i in range(ATT_TILES)], axis=1)
            out.append(softmax_update(g, lambda g=g: s_ref[g], vt, carry[g]))
        return tuple(out)

    m_far = lax.fori_loop(0, (nt + 1) // ATT_TILES, far_step, (jnp.full((1, GROUP * QB), NEG_BIG, F32),) * N_KV)

    near = (jnp.maximum(nt - 2, 0), nt - 1)

    def near_pick(tile):
        r0 = pl.multiple_of(tile * QB, QB)
        key = keys_ref[pl.ds(r0, QB), :]
        return jnp.where(key >= thr, jnp.where(admissible(r0, QB), 0.0, MASK_NEG), MASK_NEG)

    pick = jnp.concatenate([jnp.where(nt >= 2, near_pick(near[0]), MASK_NEG), near_pick(near[1])], axis=0).astype(BF16)
    for g in range(N_KV):
        kk = jnp.concatenate([kb_ref[g, pl.ds(pl.multiple_of(tl * QB, QB), QB), :] for tl in near], axis=0)
        bias = jnp.concatenate([bias_ref[g, 0], bias_ref[g, 1]], axis=0)
        s = jnp.dot(jnp.concatenate([pick, kk], axis=1), q_aug[g], preferred_element_type=F32) + bias
        vt = jnp.concatenate([vt_ref[tl, g] for tl in near], axis=1)
        softmax_update(g, lambda s=s: s, vt, m_far[g])

    heads = []
    for g in range(N_KV):
        og = acc_ref[g, 0:HEAD_DIM, :] / acc_ref[g, HEAD_DIM:HEAD_DIM + 1, :]
        heads += [og[:, hh * QB:(hh + 1) * QB] for hh in range(GROUP)]
    o_ref[...] = jnp.concatenate(heads, axis=0).T


def _dsa_key_rows(n_qb, tile_base):
    return -(-(tile_base + n_qb) // SCORE_TILES) * SCORE_TILES * QB


def _dsa(qt, qit, wit, k, v, ki, bias, *, n_qb, q_base, tile_base, l_valid):
    b, lk, _ = k.shape
    assert lk == _dsa_key_rows(n_qb, tile_base)
    n_tiles_pad = lk // QB
    topk = min(TOPK_MAX, l_valid // 4)
    body = functools.partial(_dsa_body, q_base=q_base, tile_base=tile_base, l_valid=l_valid, topk=topk,
                             idx_bits=max(1, (lk - 1).bit_length()), n_tiles_pad=n_tiles_pad)
    tok = lambda bi, qi: (0, bi * n_qb + qi)
    per_batch = lambda bi, qi: (bi, 0, 0)
    return pl.pallas_call(
        body,
        grid=(b, n_qb),
        in_specs=[pl.BlockSpec((WIDTH_A, QB), tok),
                  pl.BlockSpec((IQ_W, QB), tok),
                  pl.BlockSpec((H_IDX, QB), tok),
                  pl.BlockSpec((None, lk, KV_W), per_batch),
                  pl.BlockSpec((None, lk, KV_W), per_batch),
                  pl.BlockSpec((None, lk, D_IDX), per_batch),
                  _resident((N_KV, 2, QB, GROUP * QB))],
        out_specs=pl.BlockSpec((QB, WIDTH_A), lambda bi, qi: (bi * n_qb + qi, 0)),
        out_shape=jax.ShapeDtypeStruct((b * n_qb * QB, WIDTH_A), F32),
        scratch_shapes=[pltpu.VMEM((lk, QB), I32),
                        pltpu.VMEM((N_KV, lk, LANES), BF16),
                        pltpu.VMEM((lk, D_IDX), BF16),
                        pltpu.VMEM((n_tiles_pad, N_KV, VT_ROWS, QB), BF16),
                        pltpu.VMEM((N_KV, VT_ROWS, GROUP * QB), F32),
                        pltpu.VMEM((N_KV, ATT_TILES * QB, GROUP * QB), F32)],
        compiler_params=_params("arbitrary", "arbitrary"),
        name="dsa_attention",
    )(qt, qit, wit, k, v, ki, bias)


def _split3(x):
    hi = x.astype(BF16)
    r1 = x - hi.astype(F32)
    mid = r1.astype(BF16)
    return hi, mid, (r1 - mid.astype(F32)).astype(BF16)


def _hgrn_conv_body(pbc_ref, lbl_ref, gn_ref, cw_ref, cb_ref, lng_ref, lnb_ref, s0_ref, cbuf_ref,
                    obc_ref, sout_ref, cout_ref,
                    st_ref, ubuf_ref, xbuf_ref, r_ref, a_ref, qs_ref, kk_ref, dd_ref,
                    kmask_ref, ut_ref, oi_ref, *, layer, tm):
    t = pl.program_id(1)
    n_sub = tm // SUB
    hw = H_B * DK_B
    ri = lax.broadcasted_iota(I32, (LANES, LANES), 0)
    ci = lax.broadcasted_iota(I32, (LANES, LANES), 1)
    bd = jnp.where((ri // DK_B) == (ci // DK_B), 1.0, 0.0)
    bd16 = bd.astype(BF16)

    def pair(pp):
        return slice(pp * LANES, (pp + 1) * LANES)

    @pl.when((pl.program_id(0) == 0) & (t == 0))
    def _():
        kmask_ref[...] = jnp.zeros(kmask_ref.shape, BF16)

    @pl.when(t == 0)
    def _():
        for pp in range(N_PAIR):
            s_bd = jnp.concatenate([s0_ref[pair(pp), :]] * HEADS_PER_PAIR, axis=1) * bd
            st_ref[pp] = s_bd.T
        ubuf_ref[0:HALO - (CONV_W - 1), :] = jnp.zeros((HALO - (CONV_W - 1), C_CONV), F32)
        ubuf_ref[HALO - (CONV_W - 1):HALO, :] = cbuf_ref[...]

    @pl.when(t > 0)
    def _():
        ubuf_ref[0:HALO, :] = ubuf_ref[tm:tm + HALO, :]

    lg = lbl_ref[...]
    e = jnp.exp(lg - lg.max(axis=0, keepdims=True))
    p = e / e.sum(axis=0, keepdims=True)
    lb = jnp.zeros((1, hw), F32)
    for m in range(1, layer + 1):
        lb = lb + p[m:m + 1, :]

    qraw = pbc_ref[:, 0:hw]
    z = pbc_ref[:, hw:2 * hw]
    qs_ref[...] = qraw * _sigmoid(qraw)
    log_sig = -(jnp.maximum(-z, 0.0) + jnp.log(1.0 + jnp.exp(-jnp.abs(z))))
    a1 = jnp.log(lb)
    b1 = jnp.log1p(-lb) + log_sig
    log_f = jnp.maximum(a1, b1) + jnp.log(1.0 + jnp.exp(-jnp.abs(a1 - b1)))
    kk_ref[...] = (1.0 - lb) * _sigmoid(-z)

    tr = lax.broadcasted_iota(I32, (2 * tm, tm), 0)
    tc = lax.broadcasted_iota(I32, (2 * tm, tm), 1)
    tr_tok = jnp.where(tr < tm, tr, tr - tm)
    ones_mat = jnp.where((tr_tok // SUB) == (tc // SUB), jnp.where(tr < tm, jnp.where(tc <= tr_tok, 1.0, 0.0), 1.0), 0.0)
    ones_mat = ones_mat.astype(BF16)
    cum = sum(jnp.dot(ones_mat, part, preferred_element_type=F32) for part in _split3(log_f))
    a_loc, a_end = cum[0:tm], cum[tm:2 * tm]
    a_ref[...] = a_loc
    dd_ref[...] = jnp.exp(a_end)
    qd = (qs_ref[...] * jnp.exp(a_loc)).astype(BF16)
    kd = (kk_ref[...] * jnp.exp(a_end - a_loc)).astype(BF16)
    for s in range(n_sub):
        for pp in range(N_PAIR):
            kmask_ref[pp, s * SUB:(s + 1) * SUB, s * LANES:(s + 1) * LANES] = kd[s * SUB:(s + 1) * SUB, pair(pp)]

    ii = lax.broadcasted_iota(I32, (SUB, hw), 0)

    def build_x(s, carry):
        r0 = pl.multiple_of(s * SUB, SUB)
        a_s = a_ref[pl.ds(r0, SUB), :]
        q_s = qs_ref[pl.ds(r0, SUB), :]
        k_s = kk_ref[pl.ds(r0, SUB), :]
        for j in range(SUB):
            dec = jnp.exp(jnp.where(ii >= j, a_s - a_s[j:j + 1], -jnp.inf))
            xbuf_ref[s, j * SUB:(j + 1) * SUB, :] = (q_s * k_s[j:j + 1] * dec).astype(BF16)
        return carry

    lax.fori_loop(0, n_sub, build_x, 0)
    for pp in range(N_PAIR):
        x = xbuf_ref[:, :, pair(pp)].reshape(n_sub * SUB * SUB, LANES)
        r_ref[:, :, pair(pp)] = jnp.dot(x, bd16, preferred_element_type=F32).reshape(n_sub, SUB * SUB, LANES)

    def intra(s, carry):
        r0 = pl.multiple_of(s * SUB, SUB)
        v_s = pbc_ref[pl.ds(r0, SUB), 2 * hw:3 * hw]
        o = jnp.zeros((SUB, hw), F32)
        for j in range(SUB):
            o = o + r_ref[s, j * SUB:(j + 1) * SUB, :] * v_s[j:j + 1]
        oi_ref[pl.ds(r0, SUB), :] = o
        return carry

    lax.fori_loop(0, n_sub, intra, 0)

    own_lanes = [jnp.where(lax.broadcasted_iota(I32, (DV_B, LANES), 1) // DK_B == hh, 1.0, 0.0)
                 for hh in range(HEADS_PER_PAIR)]
    o_pairs = []
    for pp in range(N_PAIR):
        val = pbc_ref[:, 2 * hw + pp * LANES:2 * hw + (pp + 1) * LANES].astype(BF16)
        ut_ref[pp] = lax.dot_general(val, kmask_ref[pp], TN_DIMS, preferred_element_type=F32)
        slabs = [st_ref[pp, hh * DV_B:(hh + 1) * DV_B, :] for hh in range(HEADS_PER_PAIR)]
        reads = []
        for s in range(n_sub):
            st16 = jnp.concatenate(slabs, axis=0).astype(BF16)
            reads.append(lax.dot_general(qd[s * SUB:(s + 1) * SUB, pair(pp)], st16, NT_DIMS, preferred_element_type=F32))
            for hh in range(HEADS_PER_PAIR):
                blk = (pp, slice(hh * DV_B, (hh + 1) * DV_B), slice(s * LANES, (s + 1) * LANES))
                slabs[hh] = slabs[hh] * dd_ref[s * SUB:s * SUB + 1, pair(pp)] + ut_ref[blk] * own_lanes[hh]
        for hh in range(HEADS_PER_PAIR):
            st_ref[pp, hh * DV_B:(hh + 1) * DV_B, :] = slabs[hh]
        o_pairs.append(jnp.concatenate(reads, axis=0))
    o = oi_ref[...] + jnp.concatenate(o_pairs, axis=1)

    sq = (o * o).astype(BF16)
    ms = jnp.concatenate([jnp.dot(sq[:, pair(pp)], bd16, preferred_element_type=F32) for pp in range(N_PAIR)], axis=1)
    ms = ms * (1.0 / DV_B)
    gate = pbc_ref[:, 3 * hw:4 * hw]
    obc_ref[:, 0:WIDTH_B] = o * lax.rsqrt(ms + EPS) * gn_ref[...] * (gate * _sigmoid(gate))

    c0 = 4 * hw
    u = pbc_ref[:, c0:c0 + C_CONV] * _sigmoid(pbc_ref[:, c0 + C_CONV:c0 + 2 * C_CONV])
    ubuf_ref[HALO:HALO + tm, :] = u
    y = jnp.zeros((tm, C_CONV), F32)
    first = HALO - (CONV_W - 1)
    for r in range(SUBLANES):
        taps = range(r, CONV_W, SUBLANES)
        win = ubuf_ref[pl.ds(first + r, tm + SUBLANES * (len(taps) - 1)), :]
        for n, w in enumerate(taps):
            y = y + win[SUBLANES * n:SUBLANES * n + tm] * cw_ref[w:w + 1, :]
    y = y + cb_ref[...]
    yc = y - jnp.mean(y, axis=-1, keepdims=True)
    ln = yc * lax.rsqrt(jnp.mean(yc * yc, axis=-1, keepdims=True) + EPS) * lng_ref[...] + lnb_ref[...]
    obc_ref[:, WIDTH_B:WIDTH_BC] = ln * _sigmoid(ln)

    @pl.when(t == pl.num_programs(1) - 1)
    def _():
        for pp in range(N_PAIR):
            s_bd = st_ref[pp].T
            s_new = s_bd[:, 0:DV_B]
            for hh in range(1, HEADS_PER_PAIR):
                s_new = s_new + s_bd[:, hh * DV_B:(hh + 1) * DV_B]
            sout_ref[pair(pp), :] = s_new
        cout_ref[...] = ubuf_ref[tm + HALO - (CONV_W - 1):tm + HALO, :]


def _hgrn_conv(pbc, lb_logits, gn, cw, cb, lng, lnb, s0, cbuf, *, layer, seq, tm):
    b = s0.shape[0]
    n_t = seq // tm
    n_sub = tm // SUB
    hw = H_B * DK_B
    body = functools.partial(_hgrn_conv_body, layer=layer, tm=tm)
    per_batch = lambda bi, ti: (bi, 0, 0)
    tok = lambda bi, ti: (bi * n_t + ti, 0)
    return pl.pallas_call(
        body,
        grid=(b, n_t),
        in_specs=[pl.BlockSpec((tm, BC_W), tok),
                  _resident((DEPTH, hw)),
                  _resident((1, hw)),
                  _resident((CONV_W, C_CONV)),
                  _resident((1, C_CONV)),
                  _resident((1, C_CONV)),
                  _resident((1, C_CONV)),
                  pl.BlockSpec((None, hw, DV_B), per_batch),
                  pl.BlockSpec((None, CONV_W - 1, C_CONV), per_batch)],
        out_specs=[pl.BlockSpec((tm, WIDTH_BC), tok),
                   pl.BlockSpec((None, hw, DV_B), per_batch),
                   pl.BlockSpec((None, CONV_W - 1, C_CONV), per_batch)],
        out_shape=[jax.ShapeDtypeStruct((b * seq, WIDTH_BC), F32),
                   jax.ShapeDtypeStruct((b, hw, DV_B), F32),
                   jax.ShapeDtypeStruct((b, CONV_W - 1, C_CONV), F32)],
        scratch_shapes=[pltpu.VMEM((N_PAIR, LANES, LANES), F32),
                        pltpu.VMEM((tm + HALO, C_CONV), F32),
                        pltpu.VMEM((n_sub, SUB * SUB, hw), BF16),
                        pltpu.VMEM((n_sub, SUB * SUB, hw), F32),
                        pltpu.VMEM((tm, hw), F32),
                        pltpu.VMEM((tm, hw), F32),
                        pltpu.VMEM((tm, hw), F32),
                        pltpu.VMEM((tm, hw), F32),
                        pltpu.VMEM((N_PAIR, tm, n_sub * LANES), BF16),
                        pltpu.VMEM((N_PAIR, LANES, n_sub * LANES), F32),
                        pltpu.VMEM((tm, hw), F32)],
        compiler_params=_params("arbitrary", "arbitrary"),
        name="hgrn_conv",
    )(pbc, lb_logits, gn, cw, cb, lng, lnb, s0, cbuf)


def _outproj_body(x_ref, oa_ref, obc_ref, wa_ref, wbc_ref, g_ref, o_ref):
    m = jnp.dot(oa_ref[...].astype(BF16), wa_ref[...], preferred_element_type=F32)
    m = m + jnp.dot(obc_ref[...].astype(BF16), wbc_ref[...], preferred_element_type=F32)
    o_ref[...] = x_ref[...] + _rms(m, g_ref[...])


def _out_proj(x, oa, obc, wa, wbc, g, tm):
    t = x.shape[0]
    row = lambda i: (i, 0)
    return pl.pallas_call(
        _outproj_body,
        grid=(t // tm,),
        in_specs=[pl.BlockSpec((tm, D_MODEL), row),
                  pl.BlockSpec((tm, WIDTH_A), row),
                  pl.BlockSpec((tm, WIDTH_BC), row),
                  _resident((WIDTH_A, D_MODEL)),
                  _resident((WIDTH_BC, D_MODEL)),
                  _resident((1, D_MODEL))],
        out_specs=pl.BlockSpec((tm, D_MODEL), row),
        out_shape=jax.ShapeDtypeStruct((t, D_MODEL), F32),
        compiler_params=_params("arbitrary"),
        name="out_proj",
    )(x, oa, obc, wa, wbc, g)


def _split_w_in(w_in):
    sizes = (WIDTH_A, KV_W, KV_W, IQ_W, D_IDX, H_IDX)
    offs = [0]
    for s in sizes:
        offs.append(offs[-1] + s)
    wb = w_in.astype(BF16)
    q, k, v, qi, ki, wi = (wb[:, offs[n]:offs[n + 1]] for n in range(len(sizes)))
    return {"q_t": q.T, "qi_t": qi.T, "wi_t": wi.T, "k": k, "v": v, "ki": ki, "bc": wb[:, offs[-1]:]}


def _layer(x, past, lw, bias, *, layer, batch, seq, tm_tok, tm_seq):
    x = _ffn_half(x, lw["norm_ffn1"], lw["ffn1_gu"], lw["ffn1_down"], tm_tok)
    qt, qit, wit, k_new, v_new, ki_new, pbc = _in_proj(x, lw["norm_mix"][0:1], lw["w_in"], tm_tok)
    hw = H_B * DK_B
    if past is None:
        assert seq == _dsa_key_rows(seq // QB, 0)
        o_a = _dsa(qt, qit, wit, k_new.reshape(batch, seq, KV_W), v_new.reshape(batch, seq, KV_W),
                   ki_new.reshape(batch, seq, D_IDX), bias, n_qb=seq // QB, q_base=0, tile_base=0, l_valid=seq)
        s0 = jnp.zeros((batch, hw, DV_B), F32)
        cbuf = jnp.zeros((batch, CONV_W - 1, C_CONV), F32)
    else:
        k_c, v_c, ki_c, s0, cbuf = past
        past_len = k_c.shape[1]
        assert past_len % QB == 0 and seq <= QB
        l_valid = past_len + seq
        rows = _dsa_key_rows(1, past_len // QB)
        pad_k = lambda c, n, w: jnp.pad(jnp.concatenate([c.reshape(batch, past_len, w), n.reshape(batch, seq, w)], axis=1),
                                        ((0, 0), (0, rows - l_valid), (0, 0)))
        pad_q = lambda a: jnp.pad(a.reshape(a.shape[0], batch, seq), ((0, 0), (0, 0), (0, QB - seq))).reshape(a.shape[0], batch * QB)
        o_a = _dsa(pad_q(qt), pad_q(qit), pad_q(wit), pad_k(k_c, k_new, KV_W), pad_k(v_c, v_new, KV_W),
                   pad_k(ki_c, ki_new, D_IDX), bias, n_qb=1, q_base=past_len, tile_base=past_len // QB, l_valid=l_valid)
        o_a = o_a.reshape(batch, QB, WIDTH_A)[:, :seq].reshape(batch * seq, WIDTH_A)
        s0 = s0.reshape(batch, hw, DV_B)
    obc, s_new, c_new = _hgrn_conv(pbc, lw["lb_logits"], lw["gnorm"], lw["conv_w"], lw["conv_b"], lw["ln_g"], lw["ln_b"],
                                   s0, cbuf, layer=layer, seq=seq, tm=tm_seq)
    x = _out_proj(x, o_a, obc, lw["w_out_a"], lw["w_out_bc"], lw["norm_mix"][1:2], tm_tok)
    x = _ffn_half(x, lw["norm_ffn2"], lw["ffn2_gu"], lw["ffn2_down"], tm_tok)
    state = (k_new.reshape(batch, seq, N_KV, HEAD_DIM), v_new.reshape(batch, seq, N_KV, HEAD_DIM),
             ki_new.reshape(batch, seq, D_IDX), s_new.reshape(batch, H_B, DK_B, DV_B), c_new)
    return x, state


def kernel(x_prompt, x_sample, cache_attn_k, cache_attn_v, cache_idx_k, state_hgrn, state_conv, norm_ffn1, ffn1_w_gate_up, ffn1_w_down, norm_mix, w_in, w_out, rel_bias, hgrn_lb_logits, hgrn_gnorm, conv_w, conv_b, conv_ln_g, conv_ln_b, norm_ffn2, ffn2_w_gate_up, ffn2_w_down):
    bp, sp, _ = x_prompt.shape
    bs, ss, _ = x_sample.shape
    bias = _bias_tiles(rel_bias)
    xp = x_prompt.reshape(bp * sp, D_MODEL)
    xs = x_sample.reshape(bs * ss, D_MODEL)
    st_p, st_s = [], []
    for l in range(DEPTH):
        lw = {
            "norm_ffn1": norm_ffn1[l], "ffn1_gu": ffn1_w_gate_up[l].astype(BF16), "ffn1_down": ffn1_w_down[l].astype(BF16),
            "norm_mix": norm_mix[l], "w_in": _split_w_in(w_in[l]),
            "w_out_a": w_out[l, :WIDTH_A].astype(BF16), "w_out_bc": w_out[l, WIDTH_A:].astype(BF16),
            "lb_logits": hgrn_lb_logits, "gnorm": jnp.tile(hgrn_gnorm[l], H_B)[None, :],
            "conv_w": conv_w[l], "conv_b": conv_b[l][None, :], "ln_g": conv_ln_g[l][None, :], "ln_b": conv_ln_b[l][None, :],
            "norm_ffn2": norm_ffn2[l], "ffn2_gu": ffn2_w_gate_up[l].astype(BF16), "ffn2_down": ffn2_w_down[l].astype(BF16),
        }
        xp, sp_state = _layer(xp, None, lw, bias, layer=l, batch=bp, seq=sp, tm_tok=min(512, bp * sp), tm_seq=min(256, sp))
        past = (cache_attn_k[l], cache_attn_v[l], cache_idx_k[l], state_hgrn[l], state_conv[l])
        xs, ss_state = _layer(xs, past, lw, bias, layer=l, batch=bs, seq=ss, tm_tok=min(512, bs * ss), tm_seq=min(256, ss))
        st_p.append(sp_state)
        st_s.append(ss_state)
    stack = lambda sts, j: jnp.stack([s[j] for s in sts], axis=0)
    return (xp.reshape(bp, sp, D_MODEL), xs.reshape(bs, ss, D_MODEL),
            *(stack(st_p, j) for j in range(5)), *(stack(st_s, j) for j in range(5)))
```

```python
import functools
import math

import jax
import jax.numpy as jnp
from jax import lax
from jax.experimental import pallas as pl
from jax.experimental.pallas import tpu as pltpu

D_MODEL = 1024
DEPTH = 2
CHUNK = 64
CHUNK_SHIFT = 6
H_A = 8
HEAD_DIM = 64
N_KV = 2
GROUP = H_A // N_KV
H_IDX = 8
D_IDX = 64
TOPK_MAX = 256
NUM_BUCKETS = 32
MAX_DISTANCE = 128
H_B = 4
DK_B = 64
DV_B = 64
C_CONV = 256
CONV_W = 31
D_FF = 2816
EPS = 1e-6

WIDTH_A = H_A * HEAD_DIM
WIDTH_B = H_B * DV_B
WIDTH_BC = WIDTH_B + C_CONV
KV_W = N_KV * HEAD_DIM
IQ_W = H_IDX * D_IDX
BC_W = 4 * WIDTH_B + 2 * C_CONV

F32 = jnp.float32
BF16 = jnp.bfloat16
I32 = jnp.int32

LANES = 128
SUBLANES = 8
PACK16 = 2 * SUBLANES
QB = LANES
SCORE_TILES = 2
ATT_TILES = 4
VT_ROWS = HEAD_DIM + PACK16
SUB = 16
HEADS_PER_PAIR = LANES // DK_B
N_PAIR = H_B // HEADS_PER_PAIR
HALO = 32
FFN_CHUNK = 256
VMEM_LIMIT = 56 * 1024 * 1024
INT_MIN = -2 ** 31
NEG_INF_KEY = INT_MIN + 0x7FFFFF
NEG_BIG = -1e30
MASK_NEG = -2.0 ** 100
LOG2E = 1.4426950408889634

NT_DIMS = (((1,), (1,)), ((), ()))
TN_DIMS = (((0,), (0,)), ((), ()))


def _bucket_thresholds():
    half = NUM_BUCKETS // 2
    max_exact = half // 2

    def big(n):
        return min(max_exact + int(math.log(n / max_exact) / math.log(MAX_DISTANCE / max_exact) * (half - max_exact)),
                   half - 1)

    return tuple(min(n for n in range(max_exact, 4 * MAX_DISTANCE) if big(n) >= b) for b in range(max_exact + 1, half))


BUCKET_THRESHOLDS = _bucket_thresholds()


def _params(*sem):
    return pltpu.CompilerParams(dimension_semantics=sem, vmem_limit_bytes=VMEM_LIMIT)


def _resident(shape):
    return pl.BlockSpec(shape, lambda *_: (0,) * len(shape), pipeline_mode=pl.Buffered(1))


def _rms(x, g):
    return x * lax.rsqrt(jnp.mean(x * x, axis=-1, keepdims=True) + EPS) * g


def _sigmoid(x):
    return 0.5 * jnp.tanh(0.5 * x) + 0.5


def _ffn_core(x, n_ref, wgu_ref, wd_ref, acc_ref):
    h = _rms(x, n_ref[0:1, :]).astype(BF16)
    for c in range(D_FF // FFN_CHUNK):
        lo, hi = c * FFN_CHUNK, (c + 1) * FFN_CHUNK
        g = jnp.dot(h, wgu_ref[:, lo:hi], preferred_element_type=F32)
        u = jnp.dot(h, wgu_ref[:, D_FF + lo:D_FF + hi], preferred_element_type=F32)
        a = (g * _sigmoid(g) * u).astype(BF16)
        part = jnp.dot(a, wd_ref[lo:hi, :], preferred_element_type=F32)
        if c == 0:
            acc_ref[...] = part
        else:
            acc_ref[...] += part
    return x + 0.5 * _rms(acc_ref[...], n_ref[1:2, :])


def _ffn_specs():
    return [_resident((2, D_MODEL)), _resident((D_MODEL, 2 * D_FF)), _resident((D_FF, D_MODEL))]


def _ffn_inproj_body(x_ref, n_ref, wgu_ref, wd_ref, g_ref, wq_ref, wqi_ref, wwi_ref, wk_ref, wv_ref, wki_ref, wbc_ref,
                     xo_ref, qt_ref, qit_ref, wit_ref, k_ref, v_ref, ki_ref, pbc_ref, acc_ref):
    x = _ffn_core(x_ref[...], n_ref, wgu_ref, wd_ref, acc_ref)
    xo_ref[...] = x
    h = _rms(x, g_ref[...]).astype(BF16)
    qt = lax.dot_general(wq_ref[...], h, NT_DIMS, preferred_element_type=F32)
    qt_ref[...] = (qt * (HEAD_DIM ** -0.5 * LOG2E)).astype(BF16)
    qit_ref[...] = lax.dot_general(wqi_ref[...], h, NT_DIMS, preferred_element_type=F32).astype(BF16)
    wit_ref[...] = lax.dot_general(wwi_ref[...], h, NT_DIMS, preferred_element_type=F32) * (H_IDX ** -0.5)
    k_ref[...] = jnp.dot(h, wk_ref[...], preferred_element_type=F32)
    v_ref[...] = jnp.dot(h, wv_ref[...], preferred_element_type=F32)
    ki_ref[...] = jnp.dot(h, wki_ref[...], preferred_element_type=F32)
    pbc_ref[...] = jnp.dot(h, wbc_ref[...], preferred_element_type=F32)


def _ffn_in_proj(x, norms, w_gu, w_down, g, w, tm):
    t = x.shape[0]
    row = lambda i: (i, 0)
    col = lambda i: (0, i)
    return pl.pallas_call(
        _ffn_inproj_body,
        grid=(t // tm,),
        in_specs=[pl.BlockSpec((tm, D_MODEL), row),
                  *_ffn_specs(),
                  _resident((1, D_MODEL)),
                  _resident((WIDTH_A, D_MODEL)),
                  _resident((IQ_W, D_MODEL)),
                  _resident((H_IDX, D_MODEL)),
                  _resident((D_MODEL, KV_W)),
                  _resident((D_MODEL, KV_W)),
                  _resident((D_MODEL, D_IDX)),
                  _resident((D_MODEL, BC_W))],
        out_specs=[pl.BlockSpec((tm, D_MODEL), row),
                   pl.BlockSpec((WIDTH_A, tm), col),
                   pl.BlockSpec((IQ_W, tm), col),
                   pl.BlockSpec((H_IDX, tm), col),
                   pl.BlockSpec((tm, KV_W), row),
                   pl.BlockSpec((tm, KV_W), row),
                   pl.BlockSpec((tm, D_IDX), row),
                   pl.BlockSpec((tm, BC_W), row)],
        out_shape=[jax.ShapeDtypeStruct((t, D_MODEL), F32),
                   jax.ShapeDtypeStruct((WIDTH_A, t), BF16),
                   jax.ShapeDtypeStruct((IQ_W, t), BF16),
                   jax.ShapeDtypeStruct((H_IDX, t), F32),
                   jax.ShapeDtypeStruct((t, KV_W), F32),
                   jax.ShapeDtypeStruct((t, KV_W), F32),
                   jax.ShapeDtypeStruct((t, D_IDX), F32),
                   jax.ShapeDtypeStruct((t, BC_W), F32)],
        scratch_shapes=[pltpu.VMEM((tm, D_MODEL), F32)],
        compiler_params=_params("arbitrary"),
        name="ffn_in_proj",
    )(x, norms, w_gu, w_down, g, w["q_t"], w["qi_t"], w["wi_t"], w["k"], w["v"], w["ki"], w["bc"])


def _bias_body(rb_ref, o_ref):
    j = lax.broadcasted_iota(I32, (QB, QB), 0)
    i = lax.broadcasted_iota(I32, (QB, QB), 1)
    half = NUM_BUCKETS // 2
    max_exact = half // 2
    for var, off in ((0, -QB), (1, 0)):
        rel = j - i + off
        n = jnp.abs(rel)
        big = jnp.full((QB, QB), max_exact, I32)
        for thr in BUCKET_THRESHOLDS:
            big = big + jnp.where(n >= thr, 1, 0)
        bucket = jnp.where(n < max_exact, n, big) + jnp.where(rel > 0, half, 0)
        for head in range(H_A):
            g, hh = divmod(head, GROUP)
            far = rb_ref[half - 1, head]
            tile = jnp.zeros((QB, QB), F32)
            for b in range(NUM_BUCKETS):
                tile = jnp.where(bucket == b, (rb_ref[b, head] - far) * LOG2E, tile)
            o_ref[g, var, :, hh * QB:(hh + 1) * QB] = tile


def _bias_tiles(rel_bias):
    return pl.pallas_call(
        _bias_body,
        in_specs=[pl.BlockSpec(memory_space=pltpu.SMEM)],
        out_shape=jax.ShapeDtypeStruct((N_KV, 2, QB, GROUP * QB), F32),
        name="rel_bias_tiles",
    )(rel_bias)


def _dsa_body(qt_ref, qit_ref, wit_ref, k_ref, v_ref, ki_ref, bias_ref, o_ref,
              keys_ref, kb_ref, kib_ref, vt_ref, acc_ref, s_ref, *,
              q_base, tile_base, l_valid, topk, idx_bits, n_tiles_pad):
    qb = pl.program_id(1)
    nt = tile_base + qb + 1
    srows = SCORE_TILES * QB
    arows = ATT_TILES * QB

    @pl.when(qb == 0)
    def _():
        ones_rows = jnp.where(lax.broadcasted_iota(I32, (PACK16, QB), 0) == 0, 1.0, 0.0)
        for t in range(n_tiles_pad):
            rows = slice(t * QB, (t + 1) * QB)
            kt = k_ref[rows, :]
            vt = v_ref[rows, :].T
            for g in range(N_KV):
                kg = jnp.concatenate([kt[:, g * HEAD_DIM:(g + 1) * HEAD_DIM], jnp.zeros((QB, LANES - HEAD_DIM), F32)], axis=1)
                kb_ref[g, rows, :] = kg.astype(BF16)
                vt_ref[t, g] = jnp.concatenate([vt[g * HEAD_DIM:(g + 1) * HEAD_DIM, :], ones_rows], axis=0).astype(BF16)
            kib_ref[rows, :] = ki_ref[rows, :].astype(BF16)

    def admissible(r0, rows):
        k_pos = r0 + lax.broadcasted_iota(I32, (rows, QB), 0)
        q_chunk = (q_base + qb * QB + lax.broadcasted_iota(I32, (rows, QB), 1)) >> CHUNK_SHIFT
        return jnp.where(k_pos < l_valid, k_pos >> CHUNK_SHIFT, 2 ** 30) <= q_chunk

    n2 = (nt + SCORE_TILES - 1) // SCORE_TILES

    qi_pairs = [jnp.concatenate([qit_ref[h * D_IDX:(h + 1) * D_IDX, :] for h in (hp, hp + 1)], axis=1)
                for hp in range(0, H_IDX, 2)]

    def score_step(u, carry):
        r0 = pl.multiple_of(u * srows, srows)
        ki = kib_ref[pl.ds(r0, srows), :]
        sc = None
        for n, qi in enumerate(qi_pairs):
            lg = jnp.dot(ki, qi, preferred_element_type=F32)
            for hh in range(2):
                w = jnp.maximum(lg[:, hh * QB:(hh + 1) * QB], 0.0) * wit_ref[2 * n + hh:2 * n + hh + 1, :]
                sc = w if sc is None else sc + w
        sc = jnp.where(admissible(r0, srows), sc + 0.0, -jnp.inf)
        bits = pltpu.bitcast(sc, I32)
        keys_ref[pl.ds(r0, srows), :] = bits ^ ((bits >> 31) & 0x7FFFFFFF)
        return carry

    lax.fori_loop(0, n2, score_step, 0)

    crows = 2 * srows
    n4 = (n2 + 1) // 2

    @pl.when(n2 % 2 == 1)
    def _():
        keys_ref[pl.ds(pl.multiple_of(n2 * srows, srows), srows), :] = jnp.full((srows, QB), INT_MIN, I32)

    def count32(hit_fn):
        def body(t, acc):
            r0 = pl.multiple_of(t * crows, crows)
            hit = hit_fn(keys_ref[pl.ds(r0, crows), :], r0)
            return acc + hit.reshape(crows // SUBLANES, SUBLANES, QB).sum(axis=0)

        acc = lax.fori_loop(0, n4, body, jnp.zeros((SUBLANES, QB), I32))
        return acc.sum(axis=0, keepdims=True)

    def count_ge(cand):
        return count32(lambda blk, r0: jnp.where(blk >= cand, 1, 0))

    c0 = count_ge(jnp.zeros((1, QB), I32))
    thr = jnp.where(c0 >= topk, 0, INT_MIN)
    cnt = jnp.where(c0 >= topk, c0, n4 * crows)

    def bit_step(bi, carry):
        thr, cnt = carry
        cand = thr + lax.shift_left(jnp.int32(1), 30 - bi)
        c = count_ge(cand)
        return jnp.where(c >= topk, cand, thr), jnp.where(c >= topk, c, cnt)

    thr, cnt = lax.fori_loop(0, 31, bit_step, (thr, cnt))

    excess = jnp.where(thr > NEG_INF_KEY, cnt - topk, 0)

    @pl.when(jnp.max(excess) > 0)
    def _():
        need = topk - count32(lambda blk, r0: jnp.where(blk > thr, 1, 0))
        rowi = lax.broadcasted_iota(I32, (crows, QB), 0)

        def idx_step(bi, last):
            cand = last + lax.shift_left(jnp.int32(1), idx_bits - 1 - bi)
            c = count32(lambda blk, r0: jnp.where(blk == thr, jnp.where(r0 + rowi < cand, 1, 0), 0))
            return jnp.where(c < need, cand, last)

        last = lax.fori_loop(0, idx_bits, idx_step, jnp.zeros((1, QB), I32))

        def demote(t, carry):
            r0 = pl.multiple_of(t * crows, crows)
            blk = keys_ref[pl.ds(r0, crows), :]
            keys_ref[pl.ds(r0, crows), :] = jnp.where(blk == thr, jnp.where(r0 + rowi > last, thr - 1, blk), blk)
            return carry

        lax.fori_loop(0, n4, demote, 0)

    eye = jnp.where(lax.broadcasted_iota(I32, (QB, QB), 0) == lax.broadcasted_iota(I32, (QB, QB), 1), 1.0, 0.0)
    eye = jnp.concatenate([eye.astype(BF16)] * GROUP, axis=1)
    q_aug = []
    for g in range(N_KV):
        q_cat = jnp.concatenate([qt_ref[(g * GROUP + hh) * HEAD_DIM:(g * GROUP + hh + 1) * HEAD_DIM, :]
                                 for hh in range(GROUP)], axis=1)
        q_aug.append(jnp.concatenate([eye, q_cat, jnp.zeros((LANES - HEAD_DIM, GROUP * QB), BF16)], axis=0))
    acc_ref[...] = jnp.zeros(acc_ref.shape, F32)

    def softmax_update(g, load_s, vt, m_old):
        m_new = jnp.maximum(m_old, load_s().max(axis=0, keepdims=True))
        p = jnp.exp2(load_s() - m_new).astype(BF16)
        acc_ref[g] = acc_ref[g] * jnp.exp2(m_old - m_new) + jnp.dot(vt, p, preferred_element_type=F32)
        return m_new

    far_rows = (nt - 2) * QB
    row_a = lax.broadcasted_iota(I32, (arows, QB), 0)

    def far_step(u, carry):
        r0 = pl.multiple_of(u * arows, arows)
        key = keys_ref[pl.ds(r0, arows), :]
        pick = jnp.where(key >= thr, jnp.where(r0 + row_a < far_rows, 0.0, MASK_NEG), MASK_NEG).astype(BF16)
        for g in range(N_KV):
            lhs = jnp.concatenate([pick, kb_ref[g, pl.ds(r0, arows), :]], axis=1)
            s_ref[g] = jnp.dot(lhs, q_aug[g], preferred_element_type=F32)
        out = []
        for g in range(N_KV):
            vt = jnp.concatenate([vt_ref[u * ATT_TILES + i, g] for i in range(ATT_TILES)], axis=1)
            out.append(softmax_update(g, lambda g=g: s_ref[g], vt, carry[g]))
        return tuple(out)

    m_far = lax.fori_loop(0, (nt + 1) // ATT_TILES, far_step, (jnp.full((1, GROUP * QB), NEG_BIG, F32),) * N_KV)

    near = (jnp.maximum(nt - 2, 0), nt - 1)

    def near_pick(tile):
        r0 = pl.multiple_of(tile * QB, QB)
        key = keys_ref[pl.ds(r0, QB), :]
        return jnp.where(key >= thr, jnp.where(admissible(r0, QB), 0.0, MASK_NEG), MASK_NEG)

    pick = jnp.concatenate([jnp.where(nt >= 2, near_pick(near[0]), MASK_NEG), near_pick(near[1])], axis=0).astype(BF16)
    for g in range(N_KV):
        kk = jnp.concatenate([kb_ref[g, pl.ds(pl.multiple_of(tl * QB, QB), QB), :] for tl in near], axis=0)
        bias = jnp.concatenate([bias_ref[g, 0], bias_ref[g, 1]], axis=0)
        s = jnp.dot(jnp.concatenate([pick, kk], axis=1), q_aug[g], preferred_element_type=F32) + bias
        vt = jnp.concatenate([vt_ref[tl, g] for tl in near], axis=1)
        softmax_update(g, lambda s=s: s, vt, m_far[g])

    heads = []
    for g in range(N_KV):
        og = acc_ref[g, 0:HEAD_DIM, :] / acc_ref[g, HEAD_DIM:HEAD_DIM + 1, :]
        heads += [og[:, hh * QB:(hh + 1) * QB] for hh in range(GROUP)]
    o_ref[...] = jnp.concatenate(heads, axis=0).T


def _dsa_key_rows(n_qb, tile_base):
    step = 2 * SCORE_TILES
    return -(-(tile_base + n_qb) // step) * step * QB


def _dsa(qt, qit, wit, k, v, ki, bias, *, n_qb, q_base, tile_base, l_valid):
    b, lk, _ = k.shape
    assert lk == _dsa_key_rows(n_qb, tile_base)
    n_tiles_pad = lk // QB
    topk = min(TOPK_MAX, l_valid // 4)
    body = functools.partial(_dsa_body, q_base=q_base, tile_base=tile_base, l_valid=l_valid, topk=topk,
                             idx_bits=max(1, (lk - 1).bit_length()), n_tiles_pad=n_tiles_pad)
    tok = lambda bi, qi: (0, bi * n_qb + qi)
    per_batch = lambda bi, qi: (bi, 0, 0)
    return pl.pallas_call(
        body,
        grid=(b, n_qb),
        in_specs=[pl.BlockSpec((WIDTH_A, QB), tok),
                  pl.BlockSpec((IQ_W, QB), tok),
                  pl.BlockSpec((H_IDX, QB), tok),
                  pl.BlockSpec((None, lk, KV_W), per_batch),
                  pl.BlockSpec((None, lk, KV_W), per_batch),
                  pl.BlockSpec((None, lk, D_IDX), per_batch),
                  _resident((N_KV, 2, QB, GROUP * QB))],
        out_specs=pl.BlockSpec((QB, WIDTH_A), lambda bi, qi: (bi * n_qb + qi, 0)),
        out_shape=jax.ShapeDtypeStruct((b * n_qb * QB, WIDTH_A), F32),
        scratch_shapes=[pltpu.VMEM((lk, QB), I32),
                        pltpu.VMEM((N_KV, lk, LANES), BF16),
                        pltpu.VMEM((lk, D_IDX), BF16),
                        pltpu.VMEM((n_tiles_pad, N_KV, VT_ROWS, QB), BF16),
                        pltpu.VMEM((N_KV, VT_ROWS, GROUP * QB), F32),
                        pltpu.VMEM((N_KV, ATT_TILES * QB, GROUP * QB), F32)],
        compiler_params=_params("arbitrary", "arbitrary"),
        name="dsa_attention",
    )(qt, qit, wit, k, v, ki, bias)


def _split3(x):
    hi = x.astype(BF16)
    r1 = x - hi.astype(F32)
    mid = r1.astype(BF16)
    return hi, mid, (r1 - mid.astype(F32)).astype(BF16)


def _hgrn_conv_body(pbc_ref, lbl_ref, gn_ref, cw_ref, cb_ref, lng_ref, lnb_ref, s0_ref, cbuf_ref,
                    obc_ref, sout_ref, cout_ref,
                    st_ref, ubuf_ref, xbuf_ref, r_ref, a_ref, qs_ref, kk_ref, dd_ref,
                    kmask_ref, ut_ref, oi_ref, *, layer, tm):
    t = pl.program_id(1)
    n_sub = tm // SUB
    hw = H_B * DK_B
    ri = lax.broadcasted_iota(I32, (LANES, LANES), 0)
    ci = lax.broadcasted_iota(I32, (LANES, LANES), 1)
    bd = jnp.where((ri // DK_B) == (ci // DK_B), 1.0, 0.0)
    bd16 = bd.astype(BF16)

    def pair(pp):
        return slice(pp * LANES, (pp + 1) * LANES)

    @pl.when((pl.program_id(0) == 0) & (t == 0))
    def _():
        kmask_ref[...] = jnp.zeros(kmask_ref.shape, BF16)

    @pl.when(t == 0)
    def _():
        for pp in range(N_PAIR):
            s_bd = jnp.concatenate([s0_ref[pair(pp), :]] * HEADS_PER_PAIR, axis=1) * bd
            st_ref[pp] = s_bd.T
        ubuf_ref[0:HALO - (CONV_W - 1), :] = jnp.zeros((HALO - (CONV_W - 1), C_CONV), F32)
        ubuf_ref[HALO - (CONV_W - 1):HALO, :] = cbuf_ref[...]

    @pl.when(t > 0)
    def _():
        ubuf_ref[0:HALO, :] = ubuf_ref[tm:tm + HALO, :]

    lg = lbl_ref[...]
    e = jnp.exp(lg - lg.max(axis=0, keepdims=True))
    p = e / e.sum(axis=0, keepdims=True)
    lb = jnp.zeros((1, hw), F32)
    for m in range(1, layer + 1):
        lb = lb + p[m:m + 1, :]

    qraw = pbc_ref[:, 0:hw]
    z = pbc_ref[:, hw:2 * hw]
    qs_ref[...] = qraw * _sigmoid(qraw)
    log_sig = -(jnp.maximum(-z, 0.0) + jnp.log(1.0 + jnp.exp(-jnp.abs(z))))
    a1 = jnp.log(lb)
    b1 = jnp.log1p(-lb) + log_sig
    log_f = jnp.maximum(a1, b1) + jnp.log(1.0 + jnp.exp(-jnp.abs(a1 - b1)))
    kk_ref[...] = (1.0 - lb) * _sigmoid(-z)

    tr = lax.broadcasted_iota(I32, (2 * tm, tm), 0)
    tc = lax.broadcasted_iota(I32, (2 * tm, tm), 1)
    tr_tok = jnp.where(tr < tm, tr, tr - tm)
    ones_mat = jnp.where((tr_tok // SUB) == (tc // SUB), jnp.where(tr < tm, jnp.where(tc <= tr_tok, 1.0, 0.0), 1.0), 0.0)
    ones_mat = ones_mat.astype(BF16)
    cum = sum(jnp.dot(ones_mat, part, preferred_element_type=F32) for part in _split3(log_f))
    a_loc, a_end = cum[0:tm], cum[tm:2 * tm]
    a_ref[...] = a_loc
    dd_ref[...] = jnp.exp(a_end)
    qd = (qs_ref[...] * jnp.exp(a_loc)).astype(BF16)
    kd = (kk_ref[...] * jnp.exp(a_end - a_loc)).astype(BF16)
    for s in range(n_sub):
        for pp in range(N_PAIR):
            kmask_ref[pp, s * SUB:(s + 1) * SUB, s * LANES:(s + 1) * LANES] = kd[s * SUB:(s + 1) * SUB, pair(pp)]

    ii = lax.broadcasted_iota(I32, (SUB, hw), 0)

    def build_x(s, carry):
        r0 = pl.multiple_of(s * SUB, SUB)
        a_s = a_ref[pl.ds(r0, SUB), :]
        q_s = qs_ref[pl.ds(r0, SUB), :]
        k_s = kk_ref[pl.ds(r0, SUB), :]
        for j in range(SUB):
            dec = jnp.exp(jnp.where(ii >= j, a_s - a_s[j:j + 1], -jnp.inf))
            xbuf_ref[s, j * SUB:(j + 1) * SUB, :] = (q_s * k_s[j:j + 1] * dec).astype(BF16)
        return carry

    lax.fori_loop(0, n_sub, build_x, 0)
    for pp in range(N_PAIR):
        x = xbuf_ref[:, :, pair(pp)].reshape(n_sub * SUB * SUB, LANES)
        r_ref[:, :, pair(pp)] = jnp.dot(x, bd16, preferred_element_type=F32).reshape(n_sub, SUB * SUB, LANES)

    def intra(s, carry):
        r0 = pl.multiple_of(s * SUB, SUB)
        v_s = pbc_ref[pl.ds(r0, SUB), 2 * hw:3 * hw]
        o = jnp.zeros((SUB, hw), F32)
        for j in range(SUB):
            o = o + r_ref[s, j * SUB:(j + 1) * SUB, :] * v_s[j:j + 1]
        oi_ref[pl.ds(r0, SUB), :] = o
        return carry

    lax.fori_loop(0, n_sub, intra, 0)

    own_lanes = [jnp.where(lax.broadcasted_iota(I32, (DV_B, LANES), 1) // DK_B == hh, 1.0, 0.0)
                 for hh in range(HEADS_PER_PAIR)]
    o_pairs = []
    for pp in range(N_PAIR):
        val = pbc_ref[:, 2 * hw + pp * LANES:2 * hw + (pp + 1) * LANES].astype(BF16)
        ut_ref[pp] = lax.dot_general(val, kmask_ref[pp], TN_DIMS, preferred_element_type=F32)
        slabs = [st_ref[pp, hh * DV_B:(hh + 1) * DV_B, :] for hh in range(HEADS_PER_PAIR)]
        reads = []
        for s in range(n_sub):
            st16 = jnp.concatenate(slabs, axis=0).astype(BF16)
            reads.append(lax.dot_general(qd[s * SUB:(s + 1) * SUB, pair(pp)], st16, NT_DIMS, preferred_element_type=F32))
            for hh in range(HEADS_PER_PAIR):
                blk = (pp, slice(hh * DV_B, (hh + 1) * DV_B), slice(s * LANES, (s + 1) * LANES))
                slabs[hh] = slabs[hh] * dd_ref[s * SUB:s * SUB + 1, pair(pp)] + ut_ref[blk] * own_lanes[hh]
        for hh in range(HEADS_PER_PAIR):
            st_ref[pp, hh * DV_B:(hh + 1) * DV_B, :] = slabs[hh]
        o_pairs.append(jnp.concatenate(reads, axis=0))
    o = oi_ref[...] + jnp.concatenate(o_pairs, axis=1)

    sq = (o * o).astype(BF16)
    ms = jnp.concatenate([jnp.dot(sq[:, pair(pp)], bd16, preferred_element_type=F32) for pp in range(N_PAIR)], axis=1)
    ms = ms * (1.0 / DV_B)
    gate = pbc_ref[:, 3 * hw:4 * hw]
    obc_ref[:, 0:WIDTH_B] = o * lax.rsqrt(ms + EPS) * gn_ref[...] * (gate * _sigmoid(gate))

    c0 = 4 * hw
    u = pbc_ref[:, c0:c0 + C_CONV] * _sigmoid(pbc_ref[:, c0 + C_CONV:c0 + 2 * C_CONV])
    ubuf_ref[HALO:HALO + tm, :] = u
    y = jnp.zeros((tm, C_CONV), F32)
    first = HALO - (CONV_W - 1)
    for r in range(SUBLANES):
        taps = range(r, CONV_W, SUBLANES)
        win = ubuf_ref[pl.ds(first + r, tm + SUBLANES * (len(taps) - 1)), :]
        for n, w in enumerate(taps):
            y = y + win[SUBLANES * n:SUBLANES * n + tm] * cw_ref[w:w + 1, :]
    y = y + cb_ref[...]
    yc = y - jnp.mean(y, axis=-1, keepdims=True)
    ln = yc * lax.rsqrt(jnp.mean(yc * yc, axis=-1, keepdims=True) + EPS) * lng_ref[...] + lnb_ref[...]
    obc_ref[:, WIDTH_B:WIDTH_BC] = ln * _sigmoid(ln)

    @pl.when(t == pl.num_programs(1) - 1)
    def _():
        for pp in range(N_PAIR):
            s_bd = st_ref[pp].T
            s_new = s_bd[:, 0:DV_B]
            for hh in range(1, HEADS_PER_PAIR):
                s_new = s_new + s_bd[:, hh * DV_B:(hh + 1) * DV_B]
            sout_ref[pair(pp), :] = s_new
        cout_ref[...] = ubuf_ref[tm + HALO - (CONV_W - 1):tm + HALO, :]


def _hgrn_conv(pbc, lb_logits, gn, cw, cb, lng, lnb, s0, cbuf, *, layer, seq, tm):
    b = s0.shape[0]
    n_t = seq // tm
    n_sub = tm // SUB
    hw = H_B * DK_B
    body = functools.partial(_hgrn_conv_body, layer=layer, tm=tm)
    per_batch = lambda bi, ti: (bi, 0, 0)
    tok = lambda bi, ti: (bi * n_t + ti, 0)
    return pl.pallas_call(
        body,
        grid=(b, n_t),
        in_specs=[pl.BlockSpec((tm, BC_W), tok),
                  _resident((DEPTH, hw)),
                  _resident((1, hw)),
                  _resident((CONV_W, C_CONV)),
                  _resident((1, C_CONV)),
                  _resident((1, C_CONV)),
                  _resident((1, C_CONV)),
                  pl.BlockSpec((None, hw, DV_B), per_batch),
                  pl.BlockSpec((None, CONV_W - 1, C_CONV), per_batch)],
        out_specs=[pl.BlockSpec((tm, WIDTH_BC), tok),
                   pl.BlockSpec((None, hw, DV_B), per_batch),
                   pl.BlockSpec((None, CONV_W - 1, C_CONV), per_batch)],
        out_shape=[jax.ShapeDtypeStruct((b * seq, WIDTH_BC), F32),
                   jax.ShapeDtypeStruct((b, hw, DV_B), F32),
                   jax.ShapeDtypeStruct((b, CONV_W - 1, C_CONV), F32)],
        scratch_shapes=[pltpu.VMEM((N_PAIR, LANES, LANES), F32),
                        pltpu.VMEM((tm + HALO, C_CONV), F32),
                        pltpu.VMEM((n_sub, SUB * SUB, hw), BF16),
                        pltpu.VMEM((n_sub, SUB * SUB, hw), F32),
                        pltpu.VMEM((tm, hw), F32),
                        pltpu.VMEM((tm, hw), F32),
                        pltpu.VMEM((tm, hw), F32),
                        pltpu.VMEM((tm, hw), F32),
                        pltpu.VMEM((N_PAIR, tm, n_sub * LANES), BF16),
                        pltpu.VMEM((N_PAIR, LANES, n_sub * LANES), F32),
                        pltpu.VMEM((tm, hw), F32)],
        compiler_params=_params("arbitrary", "arbitrary"),
        name="hgrn_conv",
    )(pbc, lb_logits, gn, cw, cb, lng, lnb, s0, cbuf)


def _outproj_ffn_body(x_ref, oa_ref, obc_ref, wa_ref, wbc_ref, g_ref, n_ref, wgu_ref, wd_ref, o_ref, acc_ref):
    m = jnp.dot(oa_ref[...].astype(BF16), wa_ref[...], preferred_element_type=F32)
    m = m + jnp.dot(obc_ref[...].astype(BF16), wbc_ref[...], preferred_element_type=F32)
    x = x_ref[...] + _rms(m, g_ref[...])
    o_ref[...] = _ffn_core(x, n_ref, wgu_ref, wd_ref, acc_ref)


def _out_proj_ffn(x, oa, obc, wa, wbc, g, norms, w_gu, w_down, tm):
    t = x.shape[0]
    row = lambda i: (i, 0)
    return pl.pallas_call(
        _outproj_ffn_body,
        grid=(t // tm,),
        in_specs=[pl.BlockSpec((tm, D_MODEL), row),
                  pl.BlockSpec((tm, WIDTH_A), row),
                  pl.BlockSpec((tm, WIDTH_BC), row),
                  _resident((WIDTH_A, D_MODEL)),
                  _resident((WIDTH_BC, D_MODEL)),
                  _resident((1, D_MODEL)),
                  *_ffn_specs()],
        out_specs=pl.BlockSpec((tm, D_MODEL), row),
        out_shape=jax.ShapeDtypeStruct((t, D_MODEL), F32),
        scratch_shapes=[pltpu.VMEM((tm, D_MODEL), F32)],
        compiler_params=_params("arbitrary"),
        name="out_proj_ffn",
    )(x, oa, obc, wa, wbc, g, norms, w_gu, w_down)


def _split_w_in(w_in):
    sizes = (WIDTH_A, KV_W, KV_W, IQ_W, D_IDX, H_IDX)
    offs = [0]
    for s in sizes:
        offs.append(offs[-1] + s)
    wb = w_in.astype(BF16)
    q, k, v, qi, ki, wi = (wb[:, offs[n]:offs[n + 1]] for n in range(len(sizes)))
    return {"q_t": q.T, "qi_t": qi.T, "wi_t": wi.T, "k": k, "v": v, "ki": ki, "bc": wb[:, offs[-1]:]}


def _layer(x, past, lw, bias, *, layer, batch, seq, tm_tok, tm_seq):
    x, qt, qit, wit, k_new, v_new, ki_new, pbc = _ffn_in_proj(x, lw["norm_ffn1"], lw["ffn1_gu"], lw["ffn1_down"],
                                                              lw["norm_mix"][0:1], lw["w_in"], tm_tok)
    hw = H_B * DK_B
    if past is None:
        assert seq == _dsa_key_rows(seq // QB, 0)
        o_a = _dsa(qt, qit, wit, k_new.reshape(batch, seq, KV_W), v_new.reshape(batch, seq, KV_W),
                   ki_new.reshape(batch, seq, D_IDX), bias, n_qb=seq // QB, q_base=0, tile_base=0, l_valid=seq)
        s0 = jnp.zeros((batch, hw, DV_B), F32)
        cbuf = jnp.zeros((batch, CONV_W - 1, C_CONV), F32)
    else:
        k_c, v_c, ki_c, s0, cbuf = past
        past_len = k_c.shape[1]
        assert past_len % QB == 0 and seq <= QB
        l_valid = past_len + seq
        rows = _dsa_key_rows(1, past_len // QB)
        pad_k = lambda c, n, w: jnp.pad(jnp.concatenate([c.reshape(batch, past_len, w), n.reshape(batch, seq, w)], axis=1),
                                        ((0, 0), (0, rows - l_valid), (0, 0)))
        pad_q = lambda a: jnp.pad(a.reshape(a.shape[0], batch, seq), ((0, 0), (0, 0), (0, QB - seq))).reshape(a.shape[0], batch * QB)
        o_a = _dsa(pad_q(qt), pad_q(qit), pad_q(wit), pad_k(k_c, k_new, KV_W), pad_k(v_c, v_new, KV_W),
                   pad_k(ki_c, ki_new, D_IDX), bias, n_qb=1, q_base=past_len, tile_base=past_len // QB, l_valid=l_valid)
        o_a = o_a.reshape(batch, QB, WIDTH_A)[:, :seq].reshape(batch * seq, WIDTH_A)
        s0 = s0.reshape(batch, hw, DV_B)
    obc, s_new, c_new = _hgrn_conv(pbc, lw["lb_logits"], lw["gnorm"], lw["conv_w"], lw["conv_b"], lw["ln_g"], lw["ln_b"],
                                   s0, cbuf, layer=layer, seq=seq, tm=tm_seq)
    x = _out_proj_ffn(x, o_a, obc, lw["w_out_a"], lw["w_out_bc"], lw["norm_mix"][1:2],
                      lw["norm_ffn2"], lw["ffn2_gu"], lw["ffn2_down"], tm_tok)
    state = (k_new.reshape(batch, seq, N_KV, HEAD_DIM), v_new.reshape(batch, seq, N_KV, HEAD_DIM),
             ki_new.reshape(batch, seq, D_IDX), s_new.reshape(batch, H_B, DK_B, DV_B), c_new)
    return x, state


def kernel(x_prompt, x_sample, cache_attn_k, cache_attn_v, cache_idx_k, state_hgrn, state_conv, norm_ffn1, ffn1_w_gate_up, ffn1_w_down, norm_mix, w_in, w_out, rel_bias, hgrn_lb_logits, hgrn_gnorm, conv_w, conv_b, conv_ln_g, conv_ln_b, norm_ffn2, ffn2_w_gate_up, ffn2_w_down):
    bp, sp, _ = x_prompt.shape
    bs, ss, _ = x_sample.shape
    bias = _bias_tiles(rel_bias)
    xp = x_prompt.reshape(bp * sp, D_MODEL)
    xs = x_sample.reshape(bs * ss, D_MODEL)
    st_p, st_s = [], []
    for l in range(DEPTH):
        lw = {
            "norm_ffn1": norm_ffn1[l], "ffn1_gu": ffn1_w_gate_up[l].astype(BF16), "ffn1_down": ffn1_w_down[l].astype(BF16),
            "norm_mix": norm_mix[l], "w_in": _split_w_in(w_in[l]),
            "w_out_a": w_out[l, :WIDTH_A].astype(BF16), "w_out_bc": w_out[l, WIDTH_A:].astype(BF16),
            "lb_logits": hgrn_lb_logits, "gnorm": jnp.tile(hgrn_gnorm[l], H_B)[None, :],
            "conv_w": conv_w[l], "conv_b": conv_b[l][None, :], "ln_g": conv_ln_g[l][None, :], "ln_b": conv_ln_b[l][None, :],
            "norm_ffn2": norm_ffn2[l], "ffn2_gu": ffn2_w_gate_up[l].astype(BF16), "ffn2_down": ffn2_w_down[l].astype(BF16),
        }
        xp, sp_state = _layer(xp, None, lw, bias, layer=l, batch=bp, seq=sp, tm_tok=min(512, bp * sp), tm_seq=min(256, sp))
        past = (cache_attn_k[l], cache_attn_v[l], cache_idx_k[l], state_hgrn[l], state_conv[l])
        xs, ss_state = _layer(xs, past, lw, bias, layer=l, batch=bs, seq=ss, tm_tok=min(512, bs * ss), tm_seq=min(256, ss))
        st_p.append(sp_state)
        st_s.append(ss_state)
    stack = lambda sts, j: jnp.stack([s[j] for s in sts], axis=0)
    return (xp.reshape(bp, sp, D_MODEL), xs.reshape(bs, ss, D_MODEL),
            *(stack(st_p, j) for j in range(5)), *(stack(st_s, j) for j in range(5)))
```

```python
import functools
import math

import jax
import jax.numpy as jnp
from jax import lax
from jax.experimental import pallas as pl
from jax.experimental.pallas import tpu as pltpu

D_MODEL = 1024
DEPTH = 2
CHUNK = 64
CHUNK_SHIFT = 6
H_A = 8
HEAD_DIM = 64
N_KV = 2
GROUP = H_A // N_KV
H_IDX = 8
D_IDX = 64
TOPK_MAX = 256
NUM_BUCKETS = 32
MAX_DISTANCE = 128
H_B = 4
DK_B = 64
DV_B = 64
C_CONV = 256
CONV_W = 31
D_FF = 2816
EPS = 1e-6

WIDTH_A = H_A * HEAD_DIM
WIDTH_B = H_B * DV_B
WIDTH_BC = WIDTH_B + C_CONV
KV_W = N_KV * HEAD_DIM
IQ_W = H_IDX * D_IDX
BC_W = 4 * WIDTH_B + 2 * C_CONV

F32 = jnp.float32
BF16 = jnp.bfloat16
I32 = jnp.int32

LANES = 128
SUBLANES = 8
PACK16 = 2 * SUBLANES
QB = LANES
SCORE_TILES = 2
ATT_TILES = 4
VT_ROWS = HEAD_DIM + PACK16
SUB = 16
HEADS_PER_PAIR = LANES // DK_B
N_PAIR = H_B // HEADS_PER_PAIR
HALO = 32
FFN_CHUNK = 256
VMEM_LIMIT = 56 * 1024 * 1024
INT_MIN = -2 ** 31
NEG_INF_KEY = INT_MIN + 0x7FFFFF
NEG_BIG = -1e30
MASK_NEG = -2.0 ** 100
LOG2E = 1.4426950408889634

NT_DIMS = (((1,), (1,)), ((), ()))
TN_DIMS = (((0,), (0,)), ((), ()))


def _bucket_thresholds():
    half = NUM_BUCKETS // 2
    max_exact = half // 2

    def big(n):
        return min(max_exact + int(math.log(n / max_exact) / math.log(MAX_DISTANCE / max_exact) * (half - max_exact)),
                   half - 1)

    return tuple(min(n for n in range(max_exact, 4 * MAX_DISTANCE) if big(n) >= b) for b in range(max_exact + 1, half))


BUCKET_THRESHOLDS = _bucket_thresholds()


def _params(*sem):
    return pltpu.CompilerParams(dimension_semantics=sem, vmem_limit_bytes=VMEM_LIMIT)


def _resident(shape):
    return pl.BlockSpec(shape, lambda *_: (0,) * len(shape), pipeline_mode=pl.Buffered(1))


def _rms(x, g):
    return x * lax.rsqrt(jnp.mean(x * x, axis=-1, keepdims=True) + EPS) * g


def _sigmoid(x):
    return 0.5 * jnp.tanh(0.5 * x) + 0.5


def _ffn_core(x, n_ref, wgu_ref, wd_ref, acc_ref):
    h = _rms(x, n_ref[0:1, :]).astype(BF16)
    for c in range(D_FF // FFN_CHUNK):
        lo, hi = c * FFN_CHUNK, (c + 1) * FFN_CHUNK
        g = jnp.dot(h, wgu_ref[:, lo:hi], preferred_element_type=F32)
        u = jnp.dot(h, wgu_ref[:, D_FF + lo:D_FF + hi], preferred_element_type=F32)
        a = (g * _sigmoid(g) * u).astype(BF16)
        part = jnp.dot(a, wd_ref[lo:hi, :], preferred_element_type=F32)
        if c == 0:
            acc_ref[...] = part
        else:
            acc_ref[...] += part
    return x + 0.5 * _rms(acc_ref[...], n_ref[1:2, :])


def _ffn_specs():
    return [_resident((2, D_MODEL)), _resident((D_MODEL, 2 * D_FF)), _resident((D_FF, D_MODEL))]


def _ffn_inproj_body(x_ref, n_ref, wgu_ref, wd_ref, g_ref, wq_ref, wqi_ref, wwi_ref, wk_ref, wv_ref, wki_ref, wbc_ref,
                     xo_ref, qt_ref, qit_ref, wit_ref, k_ref, v_ref, ki_ref, pbc_ref, acc_ref):
    x = _ffn_core(x_ref[...], n_ref, wgu_ref, wd_ref, acc_ref)
    xo_ref[...] = x
    h = _rms(x, g_ref[...]).astype(BF16)
    qt = lax.dot_general(wq_ref[...], h, NT_DIMS, preferred_element_type=F32)
    qt_ref[...] = (qt * (HEAD_DIM ** -0.5 * LOG2E)).astype(BF16)
    qit_ref[...] = lax.dot_general(wqi_ref[...], h, NT_DIMS, preferred_element_type=F32).astype(BF16)
    wit_ref[...] = lax.dot_general(wwi_ref[...], h, NT_DIMS, preferred_element_type=F32) * (H_IDX ** -0.5)
    k_ref[...] = jnp.dot(h, wk_ref[...], preferred_element_type=F32)
    v_ref[...] = jnp.dot(h, wv_ref[...], preferred_element_type=F32)
    ki_ref[...] = jnp.dot(h, wki_ref[...], preferred_element_type=F32)
    pbc_ref[...] = jnp.dot(h, wbc_ref[...], preferred_element_type=F32)


def _ffn_in_proj(x, norms, w_gu, w_down, g, w, tm):
    t = x.shape[0]
    row = lambda i: (i, 0)
    col = lambda i: (0, i)
    return pl.pallas_call(
        _ffn_inproj_body,
        grid=(t // tm,),
        in_specs=[pl.BlockSpec((tm, D_MODEL), row),
                  *_ffn_specs(),
                  _resident((1, D_MODEL)),
                  _resident((WIDTH_A, D_MODEL)),
                  _resident((IQ_W, D_MODEL)),
                  _resident((H_IDX, D_MODEL)),
                  _resident((D_MODEL, KV_W)),
                  _resident((D_MODEL, KV_W)),
                  _resident((D_MODEL, D_IDX)),
                  _resident((D_MODEL, BC_W))],
        out_specs=[pl.BlockSpec((tm, D_MODEL), row),
                   pl.BlockSpec((WIDTH_A, tm), col),
                   pl.BlockSpec((IQ_W, tm), col),
                   pl.BlockSpec((H_IDX, tm), col),
                   pl.BlockSpec((tm, KV_W), row),
                   pl.BlockSpec((tm, KV_W), row),
                   pl.BlockSpec((tm, D_IDX), row),
                   pl.BlockSpec((tm, BC_W), row)],
        out_shape=[jax.ShapeDtypeStruct((t, D_MODEL), F32),
                   jax.ShapeDtypeStruct((WIDTH_A, t), BF16),
                   jax.ShapeDtypeStruct((IQ_W, t), BF16),
                   jax.ShapeDtypeStruct((H_IDX, t), F32),
                   jax.ShapeDtypeStruct((t, KV_W), F32),
                   jax.ShapeDtypeStruct((t, KV_W), F32),
                   jax.ShapeDtypeStruct((t, D_IDX), F32),
                   jax.ShapeDtypeStruct((t, BC_W), F32)],
        scratch_shapes=[pltpu.VMEM((tm, D_MODEL), F32)],
        compiler_params=_params("arbitrary"),
        name="ffn_in_proj",
    )(x, norms, w_gu, w_down, g, w["q_t"], w["qi_t"], w["wi_t"], w["k"], w["v"], w["ki"], w["bc"])


def _bias_body(rb_ref, o_ref):
    j = lax.broadcasted_iota(I32, (QB, QB), 0)
    i = lax.broadcasted_iota(I32, (QB, QB), 1)
    half = NUM_BUCKETS // 2
    max_exact = half // 2
    for var, off in ((0, -QB), (1, 0)):
        rel = j - i + off
        n = jnp.abs(rel)
        big = jnp.full((QB, QB), max_exact, I32)
        for thr in BUCKET_THRESHOLDS:
            big = big + jnp.where(n >= thr, 1, 0)
        bucket = jnp.where(n < max_exact, n, big) + jnp.where(rel > 0, half, 0)
        for head in range(H_A):
            g, hh = divmod(head, GROUP)
            far = rb_ref[half - 1, head]
            tile = jnp.zeros((QB, QB), F32)
            for b in range(NUM_BUCKETS):
                tile = jnp.where(bucket == b, (rb_ref[b, head] - far) * LOG2E, tile)
            o_ref[g, var, :, hh * QB:(hh + 1) * QB] = tile


def _bias_tiles(rel_bias):
    return pl.pallas_call(
        _bias_body,
        in_specs=[pl.BlockSpec(memory_space=pltpu.SMEM)],
        out_shape=jax.ShapeDtypeStruct((N_KV, 2, QB, GROUP * QB), F32),
        name="rel_bias_tiles",
    )(rel_bias)


def _dsa_body(qt_ref, qit_ref, wit_ref, k_ref, v_ref, ki_ref, bias_ref, o_ref,
              keys_ref, kb_ref, kib_ref, vt_ref, acc_ref, s_ref, *,
              q_base, tile_base, l_valid, topk, idx_bits, n_tiles_pad):
    qb = pl.program_id(1)
    nt = tile_base + qb + 1
    srows = SCORE_TILES * QB
    arows = ATT_TILES * QB

    @pl.when(qb == 0)
    def _():
        ones_rows = jnp.where(lax.broadcasted_iota(I32, (PACK16, QB), 0) == 0, 1.0, 0.0)
        for t in range(n_tiles_pad):
            rows = slice(t * QB, (t + 1) * QB)
            kt = k_ref[rows, :]
            vt = v_ref[rows, :].T
            for g in range(N_KV):
                kg = jnp.concatenate([kt[:, g * HEAD_DIM:(g + 1) * HEAD_DIM], jnp.zeros((QB, LANES - HEAD_DIM), F32)], axis=1)
                kb_ref[g, rows, :] = kg.astype(BF16)
                vt_ref[t, g] = jnp.concatenate([vt[g * HEAD_DIM:(g + 1) * HEAD_DIM, :], ones_rows], axis=0).astype(BF16)
            kib_ref[rows, :] = ki_ref[rows, :].astype(BF16)

    def admissible(r0, rows):
        k_pos = r0 + lax.broadcasted_iota(I32, (rows, QB), 0)
        q_chunk = (q_base + qb * QB + lax.broadcasted_iota(I32, (rows, QB), 1)) >> CHUNK_SHIFT
        return jnp.where(k_pos < l_valid, k_pos >> CHUNK_SHIFT, 2 ** 30) <= q_chunk

    n2 = (nt + SCORE_TILES - 1) // SCORE_TILES

    qi_pairs = [jnp.concatenate([qit_ref[h * D_IDX:(h + 1) * D_IDX, :] for h in (hp, hp + 1)], axis=1)
                for hp in range(0, H_IDX, 2)]

    def score_step(u, carry):
        r0 = pl.multiple_of(u * srows, srows)
        ki = kib_ref[pl.ds(r0, srows), :]
        sc = None
        for n, qi in enumerate(qi_pairs):
            lg = jnp.dot(ki, qi, preferred_element_type=F32)
            for hh in range(2):
                w = jnp.maximum(lg[:, hh * QB:(hh + 1) * QB], 0.0) * wit_ref[2 * n + hh:2 * n + hh + 1, :]
                sc = w if sc is None else sc + w
        sc = jnp.where(admissible(r0, srows), sc + 0.0, -jnp.inf)
        bits = pltpu.bitcast(sc, I32)
        keys_ref[pl.ds(r0, srows), :] = bits ^ ((bits >> 31) & 0x7FFFFFFF)
        return carry

    lax.fori_loop(0, n2, score_step, 0)

    crows = 2 * srows
    n4 = (n2 + 1) // 2

    @pl.when(n2 % 2 == 1)
    def _():
        keys_ref[pl.ds(pl.multiple_of(n2 * srows, srows), srows), :] = jnp.full((srows, QB), INT_MIN, I32)

    def count32(hit_fn):
        def body(t, acc):
            r0 = pl.multiple_of(t * crows, crows)
            hit = hit_fn(keys_ref[pl.ds(r0, crows), :], r0)
            return acc + hit.reshape(crows // SUBLANES, SUBLANES, QB).sum(axis=0)

        acc = lax.fori_loop(0, n4, body, jnp.zeros((SUBLANES, QB), I32))
        return acc.sum(axis=0, keepdims=True)

    def count_ge(cand):
        return count32(lambda blk, r0: jnp.where(blk >= cand, 1, 0))

    c0 = count_ge(jnp.zeros((1, QB), I32))
    thr = jnp.where(c0 >= topk, 0, INT_MIN)
    cnt = jnp.where(c0 >= topk, c0, n4 * crows)

    def bit_step(bi, carry):
        thr, cnt = carry
        cand = thr + lax.shift_left(jnp.int32(1), 30 - bi)
        c = count_ge(cand)
        return jnp.where(c >= topk, cand, thr), jnp.where(c >= topk, c, cnt)

    thr, cnt = lax.fori_loop(0, 31, bit_step, (thr, cnt))

    excess = jnp.where(thr > NEG_INF_KEY, cnt - topk, 0)

    @pl.when(jnp.max(excess) > 0)
    def _():
        need = topk - count32(lambda blk, r0: jnp.where(blk > thr, 1, 0))
        rowi = lax.broadcasted_iota(I32, (crows, QB), 0)

        def idx_step(bi, last):
            cand = last + lax.shift_left(jnp.int32(1), idx_bits - 1 - bi)
            c = count32(lambda blk, r0: jnp.where(blk == thr, jnp.where(r0 + rowi < cand, 1, 0), 0))
            return jnp.where(c < need, cand, last)

        last = lax.fori_loop(0, idx_bits, idx_step, jnp.zeros((1, QB), I32))

        def demote(t, carry):
            r0 = pl.multiple_of(t * crows, crows)
            blk = keys_ref[pl.ds(r0, crows), :]
            keys_ref[pl.ds(r0, crows), :] = jnp.where(blk == thr, jnp.where(r0 + rowi > last, thr - 1, blk), blk)
            return carry

        lax.fori_loop(0, n4, demote, 0)

    eye = jnp.where(lax.broadcasted_iota(I32, (QB, QB), 0) == lax.broadcasted_iota(I32, (QB, QB), 1), 1.0, 0.0)
    eye = jnp.concatenate([eye.astype(BF16)] * GROUP, axis=1)
    q_aug = []
    for g in range(N_KV):
        q_cat = jnp.concatenate([qt_ref[(g * GROUP + hh) * HEAD_DIM:(g * GROUP + hh + 1) * HEAD_DIM, :]
                                 for hh in range(GROUP)], axis=1)
        q_aug.append(jnp.concatenate([eye, q_cat, jnp.zeros((LANES - HEAD_DIM, GROUP * QB), BF16)], axis=0))
    acc_ref[...] = jnp.zeros(acc_ref.shape, F32)

    def softmax_update(g, load_s, vt, m_old):
        m_new = jnp.maximum(m_old, load_s().max(axis=0, keepdims=True))
        p = jnp.exp2(load_s() - m_new).astype(BF16)
        acc_ref[g] = acc_ref[g] * jnp.exp2(m_old - m_new) + jnp.dot(vt, p, preferred_element_type=F32)
        return m_new

    far_rows = (nt - 2) * QB
    row_a = lax.broadcasted_iota(I32, (arows, QB), 0)

    def far_step(u, carry):
        r0 = pl.multiple_of(u * arows, arows)
        key = keys_ref[pl.ds(r0, arows), :]
        pick = jnp.where(key >= thr, jnp.where(r0 + row_a < far_rows, 0.0, MASK_NEG), MASK_NEG).astype(BF16)
        for g in range(N_KV):
            lhs = jnp.concatenate([pick, kb_ref[g, pl.ds(r0, arows), :]], axis=1)
            s_ref[g] = jnp.dot(lhs, q_aug[g], preferred_element_type=F32)
        out = []
        for g in range(N_KV):
            vt = jnp.concatenate([vt_ref[u * ATT_TILES + i, g] for i in range(ATT_TILES)], axis=1)
            out.append(softmax_update(g, lambda g=g: s_ref[g], vt, carry[g]))
        return tuple(out)

    m_far = lax.fori_loop(0, (nt + 1) // ATT_TILES, far_step, (jnp.full((1, GROUP * QB), NEG_BIG, F32),) * N_KV)

    near = (jnp.maximum(nt - 2, 0), nt - 1)

    def near_pick(tile):
        r0 = pl.multiple_of(tile * QB, QB)
        key = keys_ref[pl.ds(r0, QB), :]
        return jnp.where(key >= thr, jnp.where(admissible(r0, QB), 0.0, MASK_NEG), MASK_NEG)

    pick = jnp.concatenate([jnp.where(nt >= 2, near_pick(near[0]), MASK_NEG), near_pick(near[1])], axis=0).astype(BF16)
    for g in range(N_KV):
        kk = jnp.concatenate([kb_ref[g, pl.ds(pl.multiple_of(tl * QB, QB), QB), :] for tl in near], axis=0)
        bias = jnp.concatenate([bias_ref[g, 0], bias_ref[g, 1]], axis=0)
        s = jnp.dot(jnp.concatenate([pick, kk], axis=1), q_aug[g], preferred_element_type=F32) + bias
        vt = jnp.concatenate([vt_ref[tl, g] for tl in near], axis=1)
        softmax_update(g, lambda s=s: s, vt, m_far[g])

    heads = []
    for g in range(N_KV):
        og = acc_ref[g, 0:HEAD_DIM, :] / acc_ref[g, HEAD_DIM:HEAD_DIM + 1, :]
        heads += [og[:, hh * QB:(hh + 1) * QB] for hh in range(GROUP)]
    o_ref[...] = jnp.concatenate(heads, axis=0).T


def _dsa_key_rows(n_qb, tile_base):
    step = 2 * SCORE_TILES
    return -(-(tile_base + n_qb) // step) * step * QB


def _dsa(qt, qit, wit, k, v, ki, bias, *, n_qb, q_base, tile_base, l_valid):
    b, lk, _ = k.shape
    assert lk == _dsa_key_rows(n_qb, tile_base)
    n_tiles_pad = lk // QB
    topk = min(TOPK_MAX, l_valid // 4)
    body = functools.partial(_dsa_body, q_base=q_base, tile_base=tile_base, l_valid=l_valid, topk=topk,
                             idx_bits=max(1, (lk - 1).bit_length()), n_tiles_pad=n_tiles_pad)
    tok = lambda bi, qi: (0, bi * n_qb + qi)
    per_batch = lambda bi, qi: (bi, 0, 0)
    return pl.pallas_call(
        body,
        grid=(b, n_qb),
        in_specs=[pl.BlockSpec((WIDTH_A, QB), tok),
                  pl.BlockSpec((IQ_W, QB), tok),
                  pl.BlockSpec((H_IDX, QB), tok),
                  pl.BlockSpec((None, lk, KV_W), per_batch),
                  pl.BlockSpec((None, lk, KV_W), per_batch),
                  pl.BlockSpec((None, lk, D_IDX), per_batch),
                  _resident((N_KV, 2, QB, GROUP * QB))],
        out_specs=pl.BlockSpec((QB, WIDTH_A), lambda bi, qi: (bi * n_qb + qi, 0)),
        out_shape=jax.ShapeDtypeStruct((b * n_qb * QB, WIDTH_A), F32),
        scratch_shapes=[pltpu.VMEM((lk, QB), I32),
                        pltpu.VMEM((N_KV, lk, LANES), BF16),
                        pltpu.VMEM((lk, D_IDX), BF16),
                        pltpu.VMEM((n_tiles_pad, N_KV, VT_ROWS, QB), BF16),
                        pltpu.VMEM((N_KV, VT_ROWS, GROUP * QB), F32),
                        pltpu.VMEM((N_KV, ATT_TILES * QB, GROUP * QB), F32)],
        compiler_params=_params("arbitrary", "arbitrary"),
        name="dsa_attention",
    )(qt, qit, wit, k, v, ki, bias)


def _split3(x):
    hi = x.astype(BF16)
    r1 = x - hi.astype(F32)
    mid = r1.astype(BF16)
    return hi, mid, (r1 - mid.astype(F32)).astype(BF16)


def _hgrn_conv_body(pbc_ref, lbl_ref, gn_ref, cw_ref, cb_ref, lng_ref, lnb_ref, s0_ref, cbuf_ref,
                    obc_ref, sout_ref, cout_ref,
                    st_ref, ubuf_ref, xbuf_ref, r_ref, a_ref, qs_ref, kk_ref, dd_ref,
                    kmask_ref, ut_ref, oi_ref, win_ref, *, layer, tm):
    t = pl.program_id(1)
    n_sub = tm // SUB
    hw = H_B * DK_B
    ri = lax.broadcasted_iota(I32, (LANES, LANES), 0)
    ci = lax.broadcasted_iota(I32, (LANES, LANES), 1)
    bd = jnp.where((ri // DK_B) == (ci // DK_B), 1.0, 0.0)
    bd16 = bd.astype(BF16)

    def pair(pp):
        return slice(pp * LANES, (pp + 1) * LANES)

    @pl.when((pl.program_id(0) == 0) & (t == 0))
    def _():
        kmask_ref[...] = jnp.zeros(kmask_ref.shape, BF16)

    @pl.when(t == 0)
    def _():
        for pp in range(N_PAIR):
            s_bd = jnp.concatenate([s0_ref[pair(pp), :]] * HEADS_PER_PAIR, axis=1) * bd
            st_ref[pp] = s_bd.T
        ubuf_ref[0:HALO - (CONV_W - 1), :] = jnp.zeros((HALO - (CONV_W - 1), C_CONV), F32)
        ubuf_ref[HALO - (CONV_W - 1):HALO, :] = cbuf_ref[...]

    @pl.when(t > 0)
    def _():
        ubuf_ref[0:HALO, :] = ubuf_ref[tm:tm + HALO, :]

    lg = lbl_ref[...]
    e = jnp.exp(lg - lg.max(axis=0, keepdims=True))
    p = e / e.sum(axis=0, keepdims=True)
    lb = jnp.zeros((1, hw), F32)
    for m in range(1, layer + 1):
        lb = lb + p[m:m + 1, :]

    qraw = pbc_ref[:, 0:hw]
    z = pbc_ref[:, hw:2 * hw]
    qs_ref[...] = qraw * _sigmoid(qraw)
    log_sig = -(jnp.maximum(-z, 0.0) + jnp.log(1.0 + jnp.exp(-jnp.abs(z))))
    a1 = jnp.log(lb)
    b1 = jnp.log1p(-lb) + log_sig
    log_f = jnp.maximum(a1, b1) + jnp.log(1.0 + jnp.exp(-jnp.abs(a1 - b1)))
    kk_ref[...] = (1.0 - lb) * _sigmoid(-z)

    tr = lax.broadcasted_iota(I32, (2 * tm, tm), 0)
    tc = lax.broadcasted_iota(I32, (2 * tm, tm), 1)
    tr_tok = jnp.where(tr < tm, tr, tr - tm)
    ones_mat = jnp.where((tr_tok // SUB) == (tc // SUB), jnp.where(tr < tm, jnp.where(tc <= tr_tok, 1.0, 0.0), 1.0), 0.0)
    ones_mat = ones_mat.astype(BF16)
    cum = sum(jnp.dot(ones_mat, part, preferred_element_type=F32) for part in _split3(log_f))
    a_loc, a_end = cum[0:tm], cum[tm:2 * tm]
    a_ref[...] = a_loc
    dd_ref[...] = jnp.exp(a_end)
    qd = (qs_ref[...] * jnp.exp(a_loc)).astype(BF16)
    kd = (kk_ref[...] * jnp.exp(a_end - a_loc)).astype(BF16)
    for s in range(n_sub):
        for pp in range(N_PAIR):
            kmask_ref[pp, s * SUB:(s + 1) * SUB, s * LANES:(s + 1) * LANES] = kd[s * SUB:(s + 1) * SUB, pair(pp)]

    ii = lax.broadcasted_iota(I32, (SUB, hw), 0)

    def build_x(s, carry):
        r0 = pl.multiple_of(s * SUB, SUB)
        a_s = a_ref[pl.ds(r0, SUB), :]
        q_s = qs_ref[pl.ds(r0, SUB), :]
        k_s = kk_ref[pl.ds(r0, SUB), :]
        for j in range(SUB):
            dec = jnp.exp(jnp.where(ii >= j, a_s - a_s[j:j + 1], -jnp.inf))
            xbuf_ref[s, j * SUB:(j + 1) * SUB, :] = (q_s * k_s[j:j + 1] * dec).astype(BF16)
        return carry

    lax.fori_loop(0, n_sub, build_x, 0)
    for pp in range(N_PAIR):
        x = xbuf_ref[:, :, pair(pp)].reshape(n_sub * SUB * SUB, LANES)
        r_ref[:, :, pair(pp)] = jnp.dot(x, bd16, preferred_element_type=F32).reshape(n_sub, SUB * SUB, LANES)

    def intra(s, carry):
        r0 = pl.multiple_of(s * SUB, SUB)
        v_s = pbc_ref[pl.ds(r0, SUB), 2 * hw:3 * hw]
        o = jnp.zeros((SUB, hw), F32)
        for j in range(SUB):
            o = o + r_ref[s, j * SUB:(j + 1) * SUB, :] * v_s[j:j + 1]
        oi_ref[pl.ds(r0, SUB), :] = o
        return carry

    lax.fori_loop(0, n_sub, intra, 0)

    own_lanes = [jnp.where(lax.broadcasted_iota(I32, (DV_B, LANES), 1) // DK_B == hh, 1.0, 0.0)
                 for hh in range(HEADS_PER_PAIR)]
    o_pairs = []
    for pp in range(N_PAIR):
        val = pbc_ref[:, 2 * hw + pp * LANES:2 * hw + (pp + 1) * LANES].astype(BF16)
        ut_ref[pp] = lax.dot_general(val, kmask_ref[pp], TN_DIMS, preferred_element_type=F32)
        slabs = [st_ref[pp, hh * DV_B:(hh + 1) * DV_B, :] for hh in range(HEADS_PER_PAIR)]
        reads = []
        for s in range(n_sub):
            st16 = jnp.concatenate(slabs, axis=0).astype(BF16)
            reads.append(lax.dot_general(qd[s * SUB:(s + 1) * SUB, pair(pp)], st16, NT_DIMS, preferred_element_type=F32))
            for hh in range(HEADS_PER_PAIR):
                blk = (pp, slice(hh * DV_B, (hh + 1) * DV_B), slice(s * LANES, (s + 1) * LANES))
                slabs[hh] = slabs[hh] * dd_ref[s * SUB:s * SUB + 1, pair(pp)] + ut_ref[blk] * own_lanes[hh]
        for hh in range(HEADS_PER_PAIR):
            st_ref[pp, hh * DV_B:(hh + 1) * DV_B, :] = slabs[hh]
        o_pairs.append(jnp.concatenate(reads, axis=0))
    o = oi_ref[...] + jnp.concatenate(o_pairs, axis=1)

    sq = (o * o).astype(BF16)
    ms = jnp.concatenate([jnp.dot(sq[:, pair(pp)], bd16, preferred_element_type=F32) for pp in range(N_PAIR)], axis=1)
    ms = ms * (1.0 / DV_B)
    gate = pbc_ref[:, 3 * hw:4 * hw]
    obc_ref[:, 0:WIDTH_B] = o * lax.rsqrt(ms + EPS) * gn_ref[...] * (gate * _sigmoid(gate))

    c0 = 4 * hw
    u = pbc_ref[:, c0:c0 + C_CONV] * _sigmoid(pbc_ref[:, c0 + C_CONV:c0 + 2 * C_CONV])
    ubuf_ref[HALO:HALO + tm, :] = u
    y = jnp.zeros((tm, C_CONV), F32)
    first = HALO - (CONV_W - 1)
    for r in range(SUBLANES):
        taps = range(r, CONV_W, SUBLANES)
        rows = tm + SUBLANES * (len(taps) - 1)
        win_ref[r, 0:rows, :] = ubuf_ref[pl.ds(first + r, rows), :]
        for n, w in enumerate(taps):
            y = y + win_ref[r, SUBLANES * n:SUBLANES * n + tm, :] * cw_ref[w:w + 1, :]
    y = y + cb_ref[...]
    yc = y - jnp.mean(y, axis=-1, keepdims=True)
    ln = yc * lax.rsqrt(jnp.mean(yc * yc, axis=-1, keepdims=True) + EPS) * lng_ref[...] + lnb_ref[...]
    obc_ref[:, WIDTH_B:WIDTH_BC] = ln * _sigmoid(ln)

    @pl.when(t == pl.num_programs(1) - 1)
    def _():
        for pp in range(N_PAIR):
            s_bd = st_ref[pp].T
            s_new = s_bd[:, 0:DV_B]
            for hh in range(1, HEADS_PER_PAIR):
                s_new = s_new + s_bd[:, hh * DV_B:(hh + 1) * DV_B]
            sout_ref[pair(pp), :] = s_new
        cout_ref[...] = ubuf_ref[tm + HALO - (CONV_W - 1):tm + HALO, :]


def _hgrn_conv(pbc, lb_logits, gn, cw, cb, lng, lnb, s0, cbuf, *, layer, seq, tm):
    b = s0.shape[0]
    n_t = seq // tm
    n_sub = tm // SUB
    hw = H_B * DK_B
    body = functools.partial(_hgrn_conv_body, layer=layer, tm=tm)
    per_batch = lambda bi, ti: (bi, 0, 0)
    tok = lambda bi, ti: (bi * n_t + ti, 0)
    return pl.pallas_call(
        body,
        grid=(b, n_t),
        in_specs=[pl.BlockSpec((tm, BC_W), tok),
                  _resident((DEPTH, hw)),
                  _resident((1, hw)),
                  _resident((CONV_W, C_CONV)),
                  _resident((1, C_CONV)),
                  _resident((1, C_CONV)),
                  _resident((1, C_CONV)),
                  pl.BlockSpec((None, hw, DV_B), per_batch),
                  pl.BlockSpec((None, CONV_W - 1, C_CONV), per_batch)],
        out_specs=[pl.BlockSpec((tm, WIDTH_BC), tok),
                   pl.BlockSpec((None, hw, DV_B), per_batch),
                   pl.BlockSpec((None, CONV_W - 1, C_CONV), per_batch)],
        out_shape=[jax.ShapeDtypeStruct((b * seq, WIDTH_BC), F32),
                   jax.ShapeDtypeStruct((b, hw, DV_B), F32),
                   jax.ShapeDtypeStruct((b, CONV_W - 1, C_CONV), F32)],
        scratch_shapes=[pltpu.VMEM((N_PAIR, LANES, LANES), F32),
                        pltpu.VMEM((tm + HALO, C_CONV), F32),
                        pltpu.VMEM((n_sub, SUB * SUB, hw), BF16),
                        pltpu.VMEM((n_sub, SUB * SUB, hw), F32),
                        pltpu.VMEM((tm, hw), F32),
                        pltpu.VMEM((tm, hw), F32),
                        pltpu.VMEM((tm, hw), F32),
                        pltpu.VMEM((tm, hw), F32),
                        pltpu.VMEM((N_PAIR, tm, n_sub * LANES), BF16),
                        pltpu.VMEM((N_PAIR, LANES, n_sub * LANES), F32),
                        pltpu.VMEM((tm, hw), F32),
                        pltpu.VMEM((SUBLANES, tm + HALO - SUBLANES, C_CONV), F32)],
        compiler_params=_params("arbitrary", "arbitrary"),
        name="hgrn_conv",
    )(pbc, lb_logits, gn, cw, cb, lng, lnb, s0, cbuf)


def _outproj_ffn_body(x_ref, oa_ref, obc_ref, wa_ref, wbc_ref, g_ref, n_ref, wgu_ref, wd_ref, o_ref, acc_ref):
    m = jnp.dot(oa_ref[...].astype(BF16), wa_ref[...], preferred_element_type=F32)
    m = m + jnp.dot(obc_ref[...].astype(BF16), wbc_ref[...], preferred_element_type=F32)
    x = x_ref[...] + _rms(m, g_ref[...])
    o_ref[...] = _ffn_core(x, n_ref, wgu_ref, wd_ref, acc_ref)


def _out_proj_ffn(x, oa, obc, wa, wbc, g, norms, w_gu, w_down, tm):
    t = x.shape[0]
    row = lambda i: (i, 0)
    return pl.pallas_call(
        _outproj_ffn_body,
        grid=(t // tm,),
        in_specs=[pl.BlockSpec((tm, D_MODEL), row),
                  pl.BlockSpec((tm, WIDTH_A), row),
                  pl.BlockSpec((tm, WIDTH_BC), row),
                  _resident((WIDTH_A, D_MODEL)),
                  _resident((WIDTH_BC, D_MODEL)),
                  _resident((1, D_MODEL)),
                  *_ffn_specs()],
        out_specs=pl.BlockSpec((tm, D_MODEL), row),
        out_shape=jax.ShapeDtypeStruct((t, D_MODEL), F32),
        scratch_shapes=[pltpu.VMEM((tm, D_MODEL), F32)],
        compiler_params=_params("arbitrary"),
        name="out_proj_ffn",
    )(x, oa, obc, wa, wbc, g, norms, w_gu, w_down)


def _split_w_in(w_in):
    sizes = (WIDTH_A, KV_W, KV_W, IQ_W, D_IDX, H_IDX)
    offs = [0]
    for s in sizes:
        offs.append(offs[-1] + s)
    wb = w_in.astype(BF16)
    q, k, v, qi, ki, wi = (wb[:, offs[n]:offs[n + 1]] for n in range(len(sizes)))
    return {"q_t": q.T, "qi_t": qi.T, "wi_t": wi.T, "k": k, "v": v, "ki": ki, "bc": wb[:, offs[-1]:]}


def _layer(x, past, lw, bias, *, layer, batch, seq, tm_tok, tm_seq):
    x, qt, qit, wit, k_new, v_new, ki_new, pbc = _ffn_in_proj(x, lw["norm_ffn1"], lw["ffn1_gu"], lw["ffn1_down"],
                                                              lw["norm_mix"][0:1], lw["w_in"], tm_tok)
    hw = H_B * DK_B
    if past is None:
        assert seq == _dsa_key_rows(seq // QB, 0)
        o_a = _dsa(qt, qit, wit, k_new.reshape(batch, seq, KV_W), v_new.reshape(batch, seq, KV_W),
                   ki_new.reshape(batch, seq, D_IDX), bias, n_qb=seq // QB, q_base=0, tile_base=0, l_valid=seq)
        s0 = jnp.zeros((batch, hw, DV_B), F32)
        cbuf = jnp.zeros((batch, CONV_W - 1, C_CONV), F32)
    else:
        k_c, v_c, ki_c, s0, cbuf = past
        past_len = k_c.shape[1]
        assert past_len % QB == 0 and seq <= QB
        l_valid = past_len + seq
        rows = _dsa_key_rows(1, past_len // QB)
        pad_k = lambda c, n, w: jnp.pad(jnp.concatenate([c.reshape(batch, past_len, w), n.reshape(batch, seq, w)], axis=1),
                                        ((0, 0), (0, rows - l_valid), (0, 0)))
        pad_q = lambda a: jnp.pad(a.reshape(a.shape[0], batch, seq), ((0, 0), (0, 0), (0, QB - seq))).reshape(a.shape[0], batch * QB)
        o_a = _dsa(pad_q(qt), pad_q(qit), pad_q(wit), pad_k(k_c, k_new, KV_W), pad_k(v_c, v_new, KV_W),
                   pad_k(ki_c, ki_new, D_IDX), bias, n_qb=1, q_base=past_len, tile_base=past_len // QB, l_valid=l_valid)
        o_a = o_a.reshape(batch, QB, WIDTH_A)[:, :seq].reshape(batch * seq, WIDTH_A)
        s0 = s0.reshape(batch, hw, DV_B)
    obc, s_new, c_new = _hgrn_conv(pbc, lw["lb_logits"], lw["gnorm"], lw["conv_w"], lw["conv_b"], lw["ln_g"], lw["ln_b"],
                                   s0, cbuf, layer=layer, seq=seq, tm=tm_seq)
    x = _out_proj_ffn(x, o_a, obc, lw["w_out_a"], lw["w_out_bc"], lw["norm_mix"][1:2],
                      lw["norm_ffn2"], lw["ffn2_gu"], lw["ffn2_down"], tm_tok)
    state = (k_new.reshape(batch, seq, N_KV, HEAD_DIM), v_new.reshape(batch, seq, N_KV, HEAD_DIM),
             ki_new.reshape(batch, seq, D_IDX), s_new.reshape(batch, H_B, DK_B, DV_B), c_new)
    return x, state


def kernel(x_prompt, x_sample, cache_attn_k, cache_attn_v, cache_idx_k, state_hgrn, state_conv, norm_ffn1, ffn1_w_gate_up, ffn1_w_down, norm_mix, w_in, w_out, rel_bias, hgrn_lb_logits, hgrn_gnorm, conv_w, conv_b, conv_ln_g, conv_ln_b, norm_ffn2, ffn2_w_gate_up, ffn2_w_down):
    bp, sp, _ = x_prompt.shape
    bs, ss, _ = x_sample.shape
    bias = _bias_tiles(rel_bias)
    xp = x_prompt.reshape(bp * sp, D_MODEL)
    xs = x_sample.reshape(bs * ss, D_MODEL)
    st_p, st_s = [], []
    for l in range(DEPTH):
        lw = {
            "norm_ffn1": norm_ffn1[l], "ffn1_gu": ffn1_w_gate_up[l].astype(BF16), "ffn1_down": ffn1_w_down[l].astype(BF16),
            "norm_mix": norm_mix[l], "w_in": _split_w_in(w_in[l]),
            "w_out_a": w_out[l, :WIDTH_A].astype(BF16), "w_out_bc": w_out[l, WIDTH_A:].astype(BF16),
            "lb_logits": hgrn_lb_logits, "gnorm": jnp.tile(hgrn_gnorm[l], H_B)[None, :],
            "conv_w": conv_w[l], "conv_b": conv_b[l][None, :], "ln_g": conv_ln_g[l][None, :], "ln_b": conv_ln_b[l][None, :],
            "norm_ffn2": norm_ffn2[l], "ffn2_gu": ffn2_w_gate_up[l].astype(BF16), "ffn2_down": ffn2_w_down[l].astype(BF16),
        }
        xp, sp_state = _layer(xp, None, lw, bias, layer=l, batch=bp, seq=sp, tm_tok=min(512, bp * sp), tm_seq=min(256, sp))
        past = (cache_attn_k[l], cache_attn_v[l], cache_idx_k[l], state_hgrn[l], state_conv[l])
        xs, ss_state = _layer(xs, past, lw, bias, layer=l, batch=bs, seq=ss, tm_tok=min(512, bs * ss), tm_seq=min(256, ss))
        st_p.append(sp_state)
        st_s.append(ss_state)
    stack = lambda sts, j: jnp.stack([s[j] for s in sts], axis=0)
    return (xp.reshape(bp, sp, D_MODEL), xs.reshape(bs, ss, D_MODEL),
            *(stack(st_p, j) for j in range(5)), *(stack(st_s, j) for j in range(5)))
```

```python
import functools
import math

import jax
import jax.numpy as jnp
from jax import lax
from jax.experimental import pallas as pl
from jax.experimental.pallas import tpu as pltpu

D_MODEL = 1024
DEPTH = 2
CHUNK = 64
CHUNK_SHIFT = 6
H_A = 8
HEAD_DIM = 64
N_KV = 2
GROUP = H_A // N_KV
H_IDX = 8
D_IDX = 64
TOPK_MAX = 256
NUM_BUCKETS = 32
MAX_DISTANCE = 128
H_B = 4
DK_B = 64
DV_B = 64
C_CONV = 256
CONV_W = 31
D_FF = 2816
EPS = 1e-6

WIDTH_A = H_A * HEAD_DIM
WIDTH_B = H_B * DV_B
WIDTH_BC = WIDTH_B + C_CONV
KV_W = N_KV * HEAD_DIM
IQ_W = H_IDX * D_IDX
BC_W = 4 * WIDTH_B + 2 * C_CONV

F32 = jnp.float32
BF16 = jnp.bfloat16
I32 = jnp.int32

LANES = 128
SUBLANES = 8
PACK16 = 2 * SUBLANES
QB = LANES
SCORE_TILES = 2
ATT_TILES = 4
VT_ROWS = HEAD_DIM + PACK16
SUB = 16
HEADS_PER_PAIR = LANES // DK_B
N_PAIR = H_B // HEADS_PER_PAIR
HALO = 32
FFN_CHUNK = 256
VMEM_LIMIT = 56 * 1024 * 1024
INT_MIN = -2 ** 31
NEG_INF_KEY = INT_MIN + 0x7FFFFF
NEG_BIG = -1e30
MASK_NEG = -2.0 ** 100
LOG2E = 1.4426950408889634

NT_DIMS = (((1,), (1,)), ((), ()))
TN_DIMS = (((0,), (0,)), ((), ()))


def _bucket_thresholds():
    half = NUM_BUCKETS // 2
    max_exact = half // 2

    def big(n):
        return min(max_exact + int(math.log(n / max_exact) / math.log(MAX_DISTANCE / max_exact) * (half - max_exact)),
                   half - 1)

    return tuple(min(n for n in range(max_exact, 4 * MAX_DISTANCE) if big(n) >= b) for b in range(max_exact + 1, half))


BUCKET_THRESHOLDS = _bucket_thresholds()


def _params(*sem):
    return pltpu.CompilerParams(dimension_semantics=sem, vmem_limit_bytes=VMEM_LIMIT)


def _resident(shape):
    return pl.BlockSpec(shape, lambda *_: (0,) * len(shape), pipeline_mode=pl.Buffered(1))


def _rms(x, g):
    return x * lax.rsqrt(jnp.mean(x * x, axis=-1, keepdims=True) + EPS) * g


def _sigmoid(x):
    return 0.5 * jnp.tanh(0.5 * x) + 0.5


def _ffn_core(x, n_ref, wgu_ref, wd_ref, acc_ref):
    h = _rms(x, n_ref[0:1, :]).astype(BF16)
    for c in range(D_FF // FFN_CHUNK):
        lo, hi = c * FFN_CHUNK, (c + 1) * FFN_CHUNK
        g = jnp.dot(h, wgu_ref[:, lo:hi], preferred_element_type=F32)
        u = jnp.dot(h, wgu_ref[:, D_FF + lo:D_FF + hi], preferred_element_type=F32)
        a = (g * _sigmoid(g) * u).astype(BF16)
        part = jnp.dot(a, wd_ref[lo:hi, :], preferred_element_type=F32)
        if c == 0:
            acc_ref[...] = part
        else:
            acc_ref[...] += part
    return x + 0.5 * _rms(acc_ref[...], n_ref[1:2, :])


def _ffn_specs():
    return [_resident((2, D_MODEL)), _resident((D_MODEL, 2 * D_FF)), _resident((D_FF, D_MODEL))]


def _ffn_body(x_ref, n_ref, wgu_ref, wd_ref, o_ref, acc_ref):
    o_ref[...] = _ffn_core(x_ref[...], n_ref, wgu_ref, wd_ref, acc_ref)


def _ffn_half(x, norms, w_gu, w_down, tm):
    t = x.shape[0]
    return pl.pallas_call(
        _ffn_body,
        grid=(t // tm,),
        in_specs=[pl.BlockSpec((tm, D_MODEL), lambda i: (i, 0)), *_ffn_specs()],
        out_specs=pl.BlockSpec((tm, D_MODEL), lambda i: (i, 0)),
        out_shape=jax.ShapeDtypeStruct((t, D_MODEL), F32),
        scratch_shapes=[pltpu.VMEM((tm, D_MODEL), F32)],
        compiler_params=_params("arbitrary"),
        name="ffn_half",
    )(x, norms, w_gu, w_down)


def _inproj_body(x_ref, g_ref, wq_ref, wqi_ref, wwi_ref, wk_ref, wv_ref, wki_ref, wbc_ref,
                 qt_ref, qit_ref, wit_ref, k_ref, v_ref, ki_ref, pbc_ref):
    h = _rms(x_ref[...], g_ref[...]).astype(BF16)
    qt = lax.dot_general(wq_ref[...], h, NT_DIMS, preferred_element_type=F32)
    qt_ref[...] = (qt * (HEAD_DIM ** -0.5 * LOG2E)).astype(BF16)
    qit_ref[...] = lax.dot_general(wqi_ref[...], h, NT_DIMS, preferred_element_type=F32).astype(BF16)
    wit_ref[...] = lax.dot_general(wwi_ref[...], h, NT_DIMS, preferred_element_type=F32) * (H_IDX ** -0.5)
    k_ref[...] = jnp.dot(h, wk_ref[...], preferred_element_type=F32)
    v_ref[...] = jnp.dot(h, wv_ref[...], preferred_element_type=F32)
    ki_ref[...] = jnp.dot(h, wki_ref[...], preferred_element_type=F32)
    pbc_ref[...] = jnp.dot(h, wbc_ref[...], preferred_element_type=F32)


def _in_proj(x, g, w, tm):
    t = x.shape[0]
    row = lambda i: (i, 0)
    col = lambda i: (0, i)
    return pl.pallas_call(
        _inproj_body,
        grid=(t // tm,),
        in_specs=[pl.BlockSpec((tm, D_MODEL), row),
                  _resident((1, D_MODEL)),
                  _resident((WIDTH_A, D_MODEL)),
                  _resident((IQ_W, D_MODEL)),
                  _resident((H_IDX, D_MODEL)),
                  _resident((D_MODEL, KV_W)),
                  _resident((D_MODEL, KV_W)),
                  _resident((D_MODEL, D_IDX)),
                  _resident((D_MODEL, BC_W))],
        out_specs=[pl.BlockSpec((WIDTH_A, tm), col),
                   pl.BlockSpec((IQ_W, tm), col),
                   pl.BlockSpec((H_IDX, tm), col),
                   pl.BlockSpec((tm, KV_W), row),
                   pl.BlockSpec((tm, KV_W), row),
                   pl.BlockSpec((tm, D_IDX), row),
                   pl.BlockSpec((tm, BC_W), row)],
        out_shape=[jax.ShapeDtypeStruct((WIDTH_A, t), BF16),
                   jax.ShapeDtypeStruct((IQ_W, t), BF16),
                   jax.ShapeDtypeStruct((H_IDX, t), F32),
                   jax.ShapeDtypeStruct((t, KV_W), F32),
                   jax.ShapeDtypeStruct((t, KV_W), F32),
                   jax.ShapeDtypeStruct((t, D_IDX), F32),
                   jax.ShapeDtypeStruct((t, BC_W), F32)],
        compiler_params=_params("arbitrary"),
        name="in_proj",
    )(x, g, w["q_t"], w["qi_t"], w["wi_t"], w["k"], w["v"], w["ki"], w["bc"])


def _bias_body(rb_ref, o_ref):
    j = lax.broadcasted_iota(I32, (QB, QB), 0)
    i = lax.broadcasted_iota(I32, (QB, QB), 1)
    half = NUM_BUCKETS // 2
    max_exact = half // 2
    for var, off in ((0, -QB), (1, 0)):
        rel = j - i + off
        n = jnp.abs(rel)
        big = jnp.full((QB, QB), max_exact, I32)
        for thr in BUCKET_THRESHOLDS:
            big = big + jnp.where(n >= thr, 1, 0)
        bucket = jnp.where(n < max_exact, n, big) + jnp.where(rel > 0, half, 0)
        for head in range(H_A):
            g, hh = divmod(head, GROUP)
            far = rb_ref[half - 1, head]
            tile = jnp.zeros((QB, QB), F32)
            for b in range(NUM_BUCKETS):
                tile = jnp.where(bucket == b, (rb_ref[b, head] - far) * LOG2E, tile)
            o_ref[g, var, :, hh * QB:(hh + 1) * QB] = tile


def _bias_tiles(rel_bias):
    return pl.pallas_call(
        _bias_body,
        in_specs=[pl.BlockSpec(memory_space=pltpu.SMEM)],
        out_shape=jax.ShapeDtypeStruct((N_KV, 2, QB, GROUP * QB), F32),
        name="rel_bias_tiles",
    )(rel_bias)


def _dsa_body(qt_ref, qit_ref, wit_ref, k_ref, v_ref, ki_ref, bias_ref, o_ref,
              keys_ref, kb_ref, kib_ref, vt_ref, acc_ref, s_ref, *,
              q_base, tile_base, l_valid, topk, idx_bits, n_tiles_pad):
    qb = pl.program_id(1)
    nt = tile_base + qb + 1
    srows = SCORE_TILES * QB
    arows = ATT_TILES * QB

    @pl.when(qb == 0)
    def _():
        ones_rows = jnp.where(lax.broadcasted_iota(I32, (PACK16, QB), 0) == 0, 1.0, 0.0)
        for t in range(n_tiles_pad):
            rows = slice(t * QB, (t + 1) * QB)
            kt = k_ref[rows, :]
            vt = v_ref[rows, :].T
            for g in range(N_KV):
                kg = jnp.concatenate([kt[:, g * HEAD_DIM:(g + 1) * HEAD_DIM], jnp.zeros((QB, LANES - HEAD_DIM), F32)], axis=1)
                kb_ref[g, rows, :] = kg.astype(BF16)
                vt_ref[t, g] = jnp.concatenate([vt[g * HEAD_DIM:(g + 1) * HEAD_DIM, :], ones_rows], axis=0).astype(BF16)
            kib_ref[rows, :] = ki_ref[rows, :].astype(BF16)

    def admissible(r0, rows):
        k_pos = r0 + lax.broadcasted_iota(I32, (rows, QB), 0)
        q_chunk = (q_base + qb * QB + lax.broadcasted_iota(I32, (rows, QB), 1)) >> CHUNK_SHIFT
        return jnp.where(k_pos < l_valid, k_pos >> CHUNK_SHIFT, 2 ** 30) <= q_chunk

    n2 = (nt + SCORE_TILES - 1) // SCORE_TILES

    qi_pairs = [jnp.concatenate([qit_ref[h * D_IDX:(h + 1) * D_IDX, :] for h in (hp, hp + 1)], axis=1)
                for hp in range(0, H_IDX, 2)]

    def score_step(u, carry):
        r0 = pl.multiple_of(u * srows, srows)
        ki = kib_ref[pl.ds(r0, srows), :]
        sc = None
        for n, qi in enumerate(qi_pairs):
            lg = jnp.dot(ki, qi, preferred_element_type=F32)
            for hh in range(2):
                w = jnp.maximum(lg[:, hh * QB:(hh + 1) * QB], 0.0) * wit_ref[2 * n + hh:2 * n + hh + 1, :]
                sc = w if sc is None else sc + w
        sc = jnp.where(admissible(r0, srows), sc + 0.0, -jnp.inf)
        bits = pltpu.bitcast(sc, I32)
        keys_ref[pl.ds(r0, srows), :] = bits ^ ((bits >> 31) & 0x7FFFFFFF)
        return carry

    lax.fori_loop(0, n2, score_step, 0)

    crows = 2 * srows
    n4 = (n2 + 1) // 2

    @pl.when(n2 % 2 == 1)
    def _():
        keys_ref[pl.ds(pl.multiple_of(n2 * srows, srows), srows), :] = jnp.full((srows, QB), INT_MIN, I32)

    def count32(hit_fn):
        def body(t, acc):
            r0 = pl.multiple_of(t * crows, crows)
            hit = hit_fn(keys_ref[pl.ds(r0, crows), :], r0)
            return acc + hit.reshape(crows // SUBLANES, SUBLANES, QB).sum(axis=0)

        acc = lax.fori_loop(0, n4, body, jnp.zeros((SUBLANES, QB), I32))
        return acc.sum(axis=0, keepdims=True)

    def count_ge(cand):
        return count32(lambda blk, r0: jnp.where(blk >= cand, 1, 0))

    c0 = count_ge(jnp.zeros((1, QB), I32))
    thr = jnp.where(c0 >= topk, 0, INT_MIN)
    cnt = jnp.where(c0 >= topk, c0, n4 * crows)

    def bit_step(bi, carry):
        thr, cnt = carry
        cand = thr + lax.shift_left(jnp.int32(1), 30 - bi)
        c = count_ge(cand)
        return jnp.where(c >= topk, cand, thr), jnp.where(c >= topk, c, cnt)

    thr, cnt = lax.fori_loop(0, 31, bit_step, (thr, cnt))

    excess = jnp.where(thr > NEG_INF_KEY, cnt - topk, 0)

    @pl.when(jnp.max(excess) > 0)
    def _():
        need = topk - count32(lambda blk, r0: jnp.where(blk > thr, 1, 0))
        rowi = lax.broadcasted_iota(I32, (crows, QB), 0)

        def idx_step(bi, last):
            cand = last + lax.shift_left(jnp.int32(1), idx_bits - 1 - bi)
            c = count32(lambda blk, r0: jnp.where(blk == thr, jnp.where(r0 + rowi < cand, 1, 0), 0))
            return jnp.where(c < need, cand, last)

        last = lax.fori_loop(0, idx_bits, idx_step, jnp.zeros((1, QB), I32))

        def demote(t, carry):
            r0 = pl.multiple_of(t * crows, crows)
            blk = keys_ref[pl.ds(r0, crows), :]
            keys_ref[pl.ds(r0, crows), :] = jnp.where(blk == thr, jnp.where(r0 + rowi > last, thr - 1, blk), blk)
            return carry

        lax.fori_loop(0, n4, demote, 0)

    eye = jnp.where(lax.broadcasted_iota(I32, (QB, QB), 0) == lax.broadcasted_iota(I32, (QB, QB), 1), 1.0, 0.0)
    eye = jnp.concatenate([eye.astype(BF16)] * GROUP, axis=1)
    q_aug = []
    for g in range(N_KV):
        q_cat = jnp.concatenate([qt_ref[(g * GROUP + hh) * HEAD_DIM:(g * GROUP + hh + 1) * HEAD_DIM, :]
                                 for hh in range(GROUP)], axis=1)
        q_aug.append(jnp.concatenate([eye, q_cat, jnp.zeros((LANES - HEAD_DIM, GROUP * QB), BF16)], axis=0))
    acc_ref[...] = jnp.zeros(acc_ref.shape, F32)

    def softmax_update(g, load_s, vt, m_old):
        m_new = jnp.maximum(m_old, load_s().max(axis=0, keepdims=True))
        p = jnp.exp2(load_s() - m_new).astype(BF16)
        acc_ref[g] = acc_ref[g] * jnp.exp2(m_old - m_new) + jnp.dot(vt, p, preferred_element_type=F32)
        return m_new

    far_rows = (nt - 2) * QB
    row_a = lax.broadcasted_iota(I32, (arows, QB), 0)

    def far_step(u, carry):
        r0 = pl.multiple_of(u * arows, arows)
        key = keys_ref[pl.ds(r0, arows), :]
        pick = jnp.where(key >= thr, jnp.where(r0 + row_a < far_rows, 0.0, MASK_NEG), MASK_NEG).astype(BF16)
        for g in range(N_KV):
            lhs = jnp.concatenate([pick, kb_ref[g, pl.ds(r0, arows), :]], axis=1)
            s_ref[g] = jnp.dot(lhs, q_aug[g], preferred_element_type=F32)
        out = []
        for g in range(N_KV):
            vt = jnp.concatenate([vt_ref[u * ATT_TILES + i, g] for i in range(ATT_TILES)], axis=1)
            out.append(softmax_update(g, lambda g=g: s_ref[g], vt, carry[g]))
        return tuple(out)

    m_far = lax.fori_loop(0, (nt + 1) // ATT_TILES, far_step, (jnp.full((1, GROUP * QB), NEG_BIG, F32),) * N_KV)

    near = (jnp.maximum(nt - 2, 0), nt - 1)

    def near_pick(tile):
        r0 = pl.multiple_of(tile * QB, QB)
        key = keys_ref[pl.ds(r0, QB), :]
        return jnp.where(key >= thr, jnp.where(admissible(r0, QB), 0.0, MASK_NEG), MASK_NEG)

    pick = jnp.concatenate([jnp.where(nt >= 2, near_pick(near[0]), MASK_NEG), near_pick(near[1])], axis=0).astype(BF16)
    for g in range(N_KV):
        kk = jnp.concatenate([kb_ref[g, pl.ds(pl.multiple_of(tl * QB, QB), QB), :] for tl in near], axis=0)
        bias = jnp.concatenate([bias_ref[g, 0], bias_ref[g, 1]], axis=0)
        s = jnp.dot(jnp.concatenate([pick, kk], axis=1), q_aug[g], preferred_element_type=F32) + bias
        vt = jnp.concatenate([vt_ref[tl, g] for tl in near], axis=1)
        softmax_update(g, lambda s=s: s, vt, m_far[g])

    heads = []
    for g in range(N_KV):
        og = acc_ref[g, 0:HEAD_DIM, :] / acc_ref[g, HEAD_DIM:HEAD_DIM + 1, :]
        heads += [og[:, hh * QB:(hh + 1) * QB] for hh in range(GROUP)]
    o_ref[...] = jnp.concatenate(heads, axis=0).T


def _dsa_key_rows(n_qb, tile_base):
    step = 2 * SCORE_TILES
    return -(-(tile_base + n_qb) // step) * step * QB


def _dsa(qt, qit, wit, k, v, ki, bias, *, n_qb, q_base, tile_base, l_valid):
    b, lk, _ = k.shape
    assert lk == _dsa_key_rows(n_qb, tile_base)
    n_tiles_pad = lk // QB
    topk = min(TOPK_MAX, l_valid // 4)
    body = functools.partial(_dsa_body, q_base=q_base, tile_base=tile_base, l_valid=l_valid, topk=topk,
                             idx_bits=max(1, (lk - 1).bit_length()), n_tiles_pad=n_tiles_pad)
    tok = lambda bi, qi: (0, bi * n_qb + qi)
    per_batch = lambda bi, qi: (bi, 0, 0)
    return pl.pallas_call(
        body,
        grid=(b, n_qb),
        in_specs=[pl.BlockSpec((WIDTH_A, QB), tok),
                  pl.BlockSpec((IQ_W, QB), tok),
                  pl.BlockSpec((H_IDX, QB), tok),
                  pl.BlockSpec((None, lk, KV_W), per_batch),
                  pl.BlockSpec((None, lk, KV_W), per_batch),
                  pl.BlockSpec((None, lk, D_IDX), per_batch),
                  _resident((N_KV, 2, QB, GROUP * QB))],
        out_specs=pl.BlockSpec((QB, WIDTH_A), lambda bi, qi: (bi * n_qb + qi, 0)),
        out_shape=jax.ShapeDtypeStruct((b * n_qb * QB, WIDTH_A), F32),
        scratch_shapes=[pltpu.VMEM((lk, QB), I32),
                        pltpu.VMEM((N_KV, lk, LANES), BF16),
                        pltpu.VMEM((lk, D_IDX), BF16),
                        pltpu.VMEM((n_tiles_pad, N_KV, VT_ROWS, QB), BF16),
                        pltpu.VMEM((N_KV, VT_ROWS, GROUP * QB), F32),
                        pltpu.VMEM((N_KV, ATT_TILES * QB, GROUP * QB), F32)],
        compiler_params=_params("arbitrary", "arbitrary"),
        name="dsa_attention",
    )(qt, qit, wit, k, v, ki, bias)


def _split3(x):
    hi = x.astype(BF16)
    r1 = x - hi.astype(F32)
    mid = r1.astype(BF16)
    return hi, mid, (r1 - mid.astype(F32)).astype(BF16)


def _hgrn_conv_body(pbc_ref, lbl_ref, gn_ref, cw_ref, cb_ref, lng_ref, lnb_ref, s0_ref, cbuf_ref,
                    obc_ref, sout_ref, cout_ref,
                    st_ref, ubuf_ref, xbuf_ref, r_ref, a_ref, qs_ref, kk_ref, dd_ref,
                    kmask_ref, ut_ref, oi_ref, win_ref, cums_ref, *, layer, tm):
    t = pl.program_id(1)
    n_sub = tm // SUB
    hw = H_B * DK_B
    ri = lax.broadcasted_iota(I32, (LANES, LANES), 0)
    ci = lax.broadcasted_iota(I32, (LANES, LANES), 1)
    bd = jnp.where((ri // DK_B) == (ci // DK_B), 1.0, 0.0)
    bd16 = bd.astype(BF16)

    def pair(pp):
        return slice(pp * LANES, (pp + 1) * LANES)

    @pl.when((pl.program_id(0) == 0) & (t == 0))
    def _():
        kmask_ref[...] = jnp.zeros(kmask_ref.shape, BF16)
        tr = lax.broadcasted_iota(I32, (2 * tm, tm), 0)
        tc = lax.broadcasted_iota(I32, (2 * tm, tm), 1)
        tr_tok = jnp.where(tr < tm, tr, tr - tm)
        same_sub = (tr_tok // SUB) == (tc // SUB)
        cums_ref[...] = jnp.where(same_sub, jnp.where(tr < tm, jnp.where(tc <= tr_tok, 1.0, 0.0), 1.0), 0.0).astype(BF16)

    @pl.when(t == 0)
    def _():
        for pp in range(N_PAIR):
            s_bd = jnp.concatenate([s0_ref[pair(pp), :]] * HEADS_PER_PAIR, axis=1) * bd
            st_ref[pp] = s_bd.T
        ubuf_ref[0:HALO - (CONV_W - 1), :] = jnp.zeros((HALO - (CONV_W - 1), C_CONV), F32)
        ubuf_ref[HALO - (CONV_W - 1):HALO, :] = cbuf_ref[...]

    @pl.when(t > 0)
    def _():
        ubuf_ref[0:HALO, :] = ubuf_ref[tm:tm + HALO, :]

    lg = lbl_ref[...]
    e = jnp.exp(lg - lg.max(axis=0, keepdims=True))
    p = e / e.sum(axis=0, keepdims=True)
    lb = jnp.zeros((1, hw), F32)
    for m in range(1, layer + 1):
        lb = lb + p[m:m + 1, :]

    qraw = pbc_ref[:, 0:hw]
    z = pbc_ref[:, hw:2 * hw]
    qs_ref[...] = qraw * _sigmoid(qraw)
    log_sig = -(jnp.maximum(-z, 0.0) + jnp.log(1.0 + jnp.exp(-jnp.abs(z))))
    a1 = jnp.log(lb)
    b1 = jnp.log1p(-lb) + log_sig
    log_f = jnp.maximum(a1, b1) + jnp.log(1.0 + jnp.exp(-jnp.abs(a1 - b1)))
    kk_ref[...] = (1.0 - lb) * _sigmoid(-z)

    cum = sum(jnp.dot(cums_ref[...], part, preferred_element_type=F32) for part in _split3(log_f))
    a_loc, a_end = cum[0:tm], cum[tm:2 * tm]
    a_ref[...] = a_loc
    dd_ref[...] = jnp.exp(a_end)
    qd = (qs_ref[...] * jnp.exp(a_loc)).astype(BF16)
    kd = (kk_ref[...] * jnp.exp(a_end - a_loc)).astype(BF16)
    for s in range(n_sub):
        for pp in range(N_PAIR):
            kmask_ref[pp, s * SUB:(s + 1) * SUB, s * LANES:(s + 1) * LANES] = kd[s * SUB:(s + 1) * SUB, pair(pp)]

    ii = lax.broadcasted_iota(I32, (SUB, hw), 0)

    def build_x(s, carry):
        r0 = pl.multiple_of(s * SUB, SUB)
        a_s = a_ref[pl.ds(r0, SUB), :]
        q_s = qs_ref[pl.ds(r0, SUB), :]
        k_s = kk_ref[pl.ds(r0, SUB), :]
        for j in range(SUB):
            dec = jnp.exp(jnp.where(ii >= j, a_s - a_s[j:j + 1], -jnp.inf))
            xbuf_ref[s, j * SUB:(j + 1) * SUB, :] = (q_s * k_s[j:j + 1] * dec).astype(BF16)
        return carry

    lax.fori_loop(0, n_sub, build_x, 0)
    for pp in range(N_PAIR):
        x = xbuf_ref[:, :, pair(pp)].reshape(n_sub * SUB * SUB, LANES)
        r_ref[:, :, pair(pp)] = jnp.dot(x, bd16, preferred_element_type=F32).reshape(n_sub, SUB * SUB, LANES)

    def intra(s, carry):
        r0 = pl.multiple_of(s * SUB, SUB)
        v_s = pbc_ref[pl.ds(r0, SUB), 2 * hw:3 * hw]
        o = jnp.zeros((SUB, hw), F32)
        for j in range(SUB):
            o = o + r_ref[s, j * SUB:(j + 1) * SUB, :] * v_s[j:j + 1]
        oi_ref[pl.ds(r0, SUB), :] = o
        return carry

    lax.fori_loop(0, n_sub, intra, 0)

    own_lanes = [jnp.where(lax.broadcasted_iota(I32, (DV_B, LANES), 1) // DK_B == hh, 1.0, 0.0)
                 for hh in range(HEADS_PER_PAIR)]
    o_pairs = []
    for pp in range(N_PAIR):
        val = pbc_ref[:, 2 * hw + pp * LANES:2 * hw + (pp + 1) * LANES].astype(BF16)
        ut_ref[pp] = lax.dot_general(val, kmask_ref[pp], TN_DIMS, preferred_element_type=F32)
        slabs = [st_ref[pp, hh * DV_B:(hh + 1) * DV_B, :] for hh in range(HEADS_PER_PAIR)]
        reads = []
        for s in range(n_sub):
            st16 = jnp.concatenate(slabs, axis=0).astype(BF16)
            reads.append(lax.dot_general(qd[s * SUB:(s + 1) * SUB, pair(pp)], st16, NT_DIMS, preferred_element_type=F32))
            for hh in range(HEADS_PER_PAIR):
                blk = (pp, slice(hh * DV_B, (hh + 1) * DV_B), slice(s * LANES, (s + 1) * LANES))
                slabs[hh] = slabs[hh] * dd_ref[s * SUB:s * SUB + 1, pair(pp)] + ut_ref[blk] * own_lanes[hh]
        for hh in range(HEADS_PER_PAIR):
            st_ref[pp, hh * DV_B:(hh + 1) * DV_B, :] = slabs[hh]
        o_pairs.append(jnp.concatenate(reads, axis=0))
    o = oi_ref[...] + jnp.concatenate(o_pairs, axis=1)

    sq = (o * o).astype(BF16)
    ms = jnp.concatenate([jnp.dot(sq[:, pair(pp)], bd16, preferred_element_type=F32) for pp in range(N_PAIR)], axis=1)
    ms = ms * (1.0 / DV_B)
    gate = pbc_ref[:, 3 * hw:4 * hw]
    obc_ref[:, 0:WIDTH_B] = o * lax.rsqrt(ms + EPS) * gn_ref[...] * (gate * _sigmoid(gate))

    c0 = 4 * hw
    u = pbc_ref[:, c0:c0 + C_CONV] * _sigmoid(pbc_ref[:, c0 + C_CONV:c0 + 2 * C_CONV])
    ubuf_ref[HALO:HALO + tm, :] = u
    y = jnp.zeros((tm, C_CONV), F32)
    first = HALO - (CONV_W - 1)
    for r in range(SUBLANES):
        taps = range(r, CONV_W, SUBLANES)
        rows = tm + SUBLANES * (len(taps) - 1)
        win_ref[r, 0:rows, :] = ubuf_ref[pl.ds(first + r, rows), :]
        for n, w in enumerate(taps):
            y = y + win_ref[r, SUBLANES * n:SUBLANES * n + tm, :] * cw_ref[w:w + 1, :]
    y = y + cb_ref[...]
    yc = y - jnp.mean(y, axis=-1, keepdims=True)
    ln = yc * lax.rsqrt(jnp.mean(yc * yc, axis=-1, keepdims=True) + EPS) * lng_ref[...] + lnb_ref[...]
    obc_ref[:, WIDTH_B:WIDTH_BC] = ln * _sigmoid(ln)

    @pl.when(t == pl.num_programs(1) - 1)
    def _():
        for pp in range(N_PAIR):
            s_bd = st_ref[pp].T
            s_new = s_bd[:, 0:DV_B]
            for hh in range(1, HEADS_PER_PAIR):
                s_new = s_new + s_bd[:, hh * DV_B:(hh + 1) * DV_B]
            sout_ref[pair(pp), :] = s_new
        cout_ref[...] = ubuf_ref[tm + HALO - (CONV_W - 1):tm + HALO, :]


def _hgrn_conv(pbc, lb_logits, gn, cw, cb, lng, lnb, s0, cbuf, *, layer, seq, tm):
    b = s0.shape[0]
    n_t = seq // tm
    n_sub = tm // SUB
    hw = H_B * DK_B
    body = functools.partial(_hgrn_conv_body, layer=layer, tm=tm)
    per_batch = lambda bi, ti: (bi, 0, 0)
    tok = lambda bi, ti: (bi * n_t + ti, 0)
    return pl.pallas_call(
        body,
        grid=(b, n_t),
        in_specs=[pl.BlockSpec((tm, BC_W), tok),
                  _resident((DEPTH, hw)),
                  _resident((1, hw)),
                  _resident((CONV_W, C_CONV)),
                  _resident((1, C_CONV)),
                  _resident((1, C_CONV)),
                  _resident((1, C_CONV)),
                  pl.BlockSpec((None, hw, DV_B), per_batch),
                  pl.BlockSpec((None, CONV_W - 1, C_CONV), per_batch)],
        out_specs=[pl.BlockSpec((tm, WIDTH_BC), tok),
                   pl.BlockSpec((None, hw, DV_B), per_batch),
                   pl.BlockSpec((None, CONV_W - 1, C_CONV), per_batch)],
        out_shape=[jax.ShapeDtypeStruct((b * seq, WIDTH_BC), F32),
                   jax.ShapeDtypeStruct((b, hw, DV_B), F32),
                   jax.ShapeDtypeStruct((b, CONV_W - 1, C_CONV), F32)],
        scratch_shapes=[pltpu.VMEM((N_PAIR, LANES, LANES), F32),
                        pltpu.VMEM((tm + HALO, C_CONV), F32),
                        pltpu.VMEM((n_sub, SUB * SUB, hw), BF16),
                        pltpu.VMEM((n_sub, SUB * SUB, hw), F32),
                        pltpu.VMEM((tm, hw), F32),
                        pltpu.VMEM((tm, hw), F32),
                        pltpu.VMEM((tm, hw), F32),
                        pltpu.VMEM((tm, hw), F32),
                        pltpu.VMEM((N_PAIR, tm, n_sub * LANES), BF16),
                        pltpu.VMEM((N_PAIR, LANES, n_sub * LANES), F32),
                        pltpu.VMEM((tm, hw), F32),
                        pltpu.VMEM((SUBLANES, tm + HALO - SUBLANES, C_CONV), F32),
                        pltpu.VMEM((2 * tm, tm), BF16)],
        compiler_params=_params("arbitrary", "arbitrary"),
        name="hgrn_conv",
    )(pbc, lb_logits, gn, cw, cb, lng, lnb, s0, cbuf)


def _outproj_ffn_body(x_ref, oa_ref, obc_ref, wa_ref, wbc_ref, g_ref, n_ref, wgu_ref, wd_ref, o_ref, acc_ref):
    m = jnp.dot(oa_ref[...].astype(BF16), wa_ref[...], preferred_element_type=F32)
    m = m + jnp.dot(obc_ref[...].astype(BF16), wbc_ref[...], preferred_element_type=F32)
    x = x_ref[...] + _rms(m, g_ref[...])
    o_ref[...] = _ffn_core(x, n_ref, wgu_ref, wd_ref, acc_ref)


def _out_proj_ffn(x, oa, obc, wa, wbc, g, norms, w_gu, w_down, tm):
    t = x.shape[0]
    row = lambda i: (i, 0)
    return pl.pallas_call(
        _outproj_ffn_body,
        grid=(t // tm,),
        in_specs=[pl.BlockSpec((tm, D_MODEL), row),
                  pl.BlockSpec((tm, WIDTH_A), row),
                  pl.BlockSpec((tm, WIDTH_BC), row),
                  _resident((WIDTH_A, D_MODEL)),
                  _resident((WIDTH_BC, D_MODEL)),
                  _resident((1, D_MODEL)),
                  *_ffn_specs()],
        out_specs=pl.BlockSpec((tm, D_MODEL), row),
        out_shape=jax.ShapeDtypeStruct((t, D_MODEL), F32),
        scratch_shapes=[pltpu.VMEM((tm, D_MODEL), F32)],
        compiler_params=_params("arbitrary"),
        name="out_proj_ffn",
    )(x, oa, obc, wa, wbc, g, norms, w_gu, w_down)


def _split_w_in(w_in):
    sizes = (WIDTH_A, KV_W, KV_W, IQ_W, D_IDX, H_IDX)
    offs = [0]
    for s in sizes:
        offs.append(offs[-1] + s)
    wb = w_in.astype(BF16)
    q, k, v, qi, ki, wi = (wb[:, offs[n]:offs[n + 1]] for n in range(len(sizes)))
    return {"q_t": q.T, "qi_t": qi.T, "wi_t": wi.T, "k": k, "v": v, "ki": ki, "bc": wb[:, offs[-1]:]}


def _layer(x, past, lw, bias, *, layer, batch, seq, tm_tok, tm_seq):
    x = _ffn_half(x, lw["norm_ffn1"], lw["ffn1_gu"], lw["ffn1_down"], tm_tok)
    qt, qit, wit, k_new, v_new, ki_new, pbc = _in_proj(x, lw["norm_mix"][0:1], lw["w_in"], tm_tok)
    hw = H_B * DK_B
    if past is None:
        assert seq == _dsa_key_rows(seq // QB, 0)
        o_a = _dsa(qt, qit, wit, k_new.reshape(batch, seq, KV_W), v_new.reshape(batch, seq, KV_W),
                   ki_new.reshape(batch, seq, D_IDX), bias, n_qb=seq // QB, q_base=0, tile_base=0, l_valid=seq)
        s0 = jnp.zeros((batch, hw, DV_B), F32)
        cbuf = jnp.zeros((batch, CONV_W - 1, C_CONV), F32)
    else:
        k_c, v_c, ki_c, s0, cbuf = past
        past_len = k_c.shape[1]
        assert past_len % QB == 0 and seq <= QB
        l_valid = past_len + seq
        rows = _dsa_key_rows(1, past_len // QB)
        pad_k = lambda c, n, w: jnp.pad(jnp.concatenate([c.reshape(batch, past_len, w), n.reshape(batch, seq, w)], axis=1),
                                        ((0, 0), (0, rows - l_valid), (0, 0)))
        pad_q = lambda a: jnp.pad(a.reshape(a.shape[0], batch, seq), ((0, 0), (0, 0), (0, QB - seq))).reshape(a.shape[0], batch * QB)
        o_a = _dsa(pad_q(qt), pad_q(qit), pad_q(wit), pad_k(k_c, k_new, KV_W), pad_k(v_c, v_new, KV_W),
                   pad_k(ki_c, ki_new, D_IDX), bias, n_qb=1, q_base=past_len, tile_base=past_len // QB, l_valid=l_valid)
        o_a = o_a.reshape(batch, QB, WIDTH_A)[:, :seq].reshape(batch * seq, WIDTH_A)
        s0 = s0.reshape(batch, hw, DV_B)
    obc, s_new, c_new = _hgrn_conv(pbc, lw["lb_logits"], lw["gnorm"], lw["conv_w"], lw["conv_b"], lw["ln_g"], lw["ln_b"],
                                   s0, cbuf, layer=layer, seq=seq, tm=tm_seq)
    x = _out_proj_ffn(x, o_a, obc, lw["w_out_a"], lw["w_out_bc"], lw["norm_mix"][1:2],
                      lw["norm_ffn2"], lw["ffn2_gu"], lw["ffn2_down"], tm_tok)
    state = (k_new.reshape(batch, seq, N_KV, HEAD_DIM), v_new.reshape(batch, seq, N_KV, HEAD_DIM),
             ki_new.reshape(batch, seq, D_IDX), s_new.reshape(batch, H_B, DK_B, DV_B), c_new)
    return x, state


def kernel(x_prompt, x_sample, cache_attn_k, cache_attn_v, cache_idx_k, state_hgrn, state_conv, norm_ffn1, ffn1_w_gate_up, ffn1_w_down, norm_mix, w_in, w_out, rel_bias, hgrn_lb_logits, hgrn_gnorm, conv_w, conv_b, conv_ln_g, conv_ln_b, norm_ffn2, ffn2_w_gate_up, ffn2_w_down):
    bp, sp, _ = x_prompt.shape
    bs, ss, _ = x_sample.shape
    bias = _bias_tiles(rel_bias)
    xp = x_prompt.reshape(bp * sp, D_MODEL)
    xs = x_sample.reshape(bs * ss, D_MODEL)
    st_p, st_s = [], []
    for l in range(DEPTH):
        lw = {
            "norm_ffn1": norm_ffn1[l], "ffn1_gu": ffn1_w_gate_up[l].astype(BF16), "ffn1_down": ffn1_w_down[l].astype(BF16),
            "norm_mix": norm_mix[l], "w_in": _split_w_in(w_in[l]),
            "w_out_a": w_out[l, :WIDTH_A].astype(BF16), "w_out_bc": w_out[l, WIDTH_A:].astype(BF16),
            "lb_logits": hgrn_lb_logits, "gnorm": jnp.tile(hgrn_gnorm[l], H_B)[None, :],
            "conv_w": conv_w[l], "conv_b": conv_b[l][None, :], "ln_g": conv_ln_g[l][None, :], "ln_b": conv_ln_b[l][None, :],
            "norm_ffn2": norm_ffn2[l], "ffn2_gu": ffn2_w_gate_up[l].astype(BF16), "ffn2_down": ffn2_w_down[l].astype(BF16),
        }
        xp, sp_state = _layer(xp, None, lw, bias, layer=l, batch=bp, seq=sp, tm_tok=min(512, bp * sp), tm_seq=min(256, sp))
        past = (cache_attn_k[l], cache_attn_v[l], cache_idx_k[l], state_hgrn[l], state_conv[l])
        xs, ss_state = _layer(xs, past, lw, bias, layer=l, batch=bs, seq=ss, tm_tok=min(512, bs * ss), tm_seq=min(256, ss))
        st_p.append(sp_state)
        st_s.append(ss_state)
    stack = lambda sts, j: jnp.stack([s[j] for s in sts], axis=0)
    return (xp.reshape(bp, sp, D_MODEL), xs.reshape(bs, ss, D_MODEL),
            *(stack(st_p, j) for j in range(5)), *(stack(st_s, j) for j in range(5)))
```
